```python
import math
import jax, jax.numpy as jnp
from jax import lax
import numpy as np

D_MODEL = 1024
BATCH = 2
SEQ = 8192
DEPTH = 2

RET_HEADS = 4
RET_DK = D_MODEL // RET_HEADS
RET_DV = 2 * RET_DK
RET_CHUNK = 128
RET_THETA = 10000.0
ATT_GROUPS = ((128, 1), (512, 4), (2048, 16))
ATT_HEADS_PER_GROUP = 4
ATT_HEADS = ATT_HEADS_PER_GROUP * len(ATT_GROUPS)
ATT_DH = 128
ATT_ROT_DIM = ATT_DH // 4
ATT_THETA = 500000.0
D_FF = ((8 * D_MODEL // 3 + 255) // 256) * 256
NORM_EPS = 1e-6

SPLIT_SIZES = (
    RET_HEADS * RET_DK,
    RET_HEADS * RET_DK,
    RET_HEADS * RET_DV,
    RET_HEADS * RET_DV,
    ATT_HEADS * ATT_DH,
    ATT_HEADS * ATT_DH,
    ATT_HEADS * ATT_DH,
    D_MODEL,
    D_MODEL,
)
D_IN = sum(SPLIT_SIZES)

kernel_name = "hybrid_retention_dilated_attn_block"


def rmsnorm(x, g):
    xf = x.astype(jnp.float32)
    y = xf * lax.rsqrt(jnp.mean(xf * xf, axis=-1, keepdims=True) + NORM_EPS)
    return (y * g.astype(jnp.float32)).astype(x.dtype)


def rotary(x, rot_dim, theta):
    s = x.shape[1]
    half = rot_dim // 2
    inv = theta ** (-jnp.arange(half, dtype=jnp.float32) / half)
    ang = jnp.arange(s, dtype=jnp.float32)[:, None] * inv[None, :]
    cos = jnp.cos(ang)[None, :, None, :]
    sin = jnp.sin(ang)[None, :, None, :]
    xr = x[..., :rot_dim].astype(jnp.float32)
    x1, x2 = xr[..., :half], xr[..., half:]
    rot = jnp.concatenate([x1 * cos - x2 * sin, x1 * sin + x2 * cos], axis=-1)
    return jnp.concatenate([rot.astype(x.dtype), x[..., rot_dim:]], axis=-1)


def retention_chunkwise(q, k, v):
    b, s, h, dk = q.shape
    dv = v.shape[-1]
    c = RET_CHUNK
    n = s // c
    lg = jnp.log(1.0 - 2.0 ** (-5.0 - jnp.arange(h, dtype=jnp.float32)))
    pos = jnp.arange(c, dtype=jnp.float32)
    rel = pos[:, None] - pos[None, :]
    dmask = jnp.where(rel[None] >= 0,
                      jnp.exp(jnp.maximum(rel, 0.0)[None] * lg[:, None, None]), 0.0)
    xi = jnp.exp((pos + 1.0)[None, :] * lg[:, None])
    zeta = jnp.exp((c - 1.0 - pos)[None, :] * lg[:, None])
    chunk_decay = jnp.exp(c * lg)[None, :, None, None]

    def chunks(t):
        return t.astype(jnp.float32).reshape(b, n, c, h, t.shape[-1]).transpose(0, 1, 3, 2, 4)

    qc, kc, vc = chunks(q), chunks(k), chunks(v)
    scores = jnp.einsum('bnhcd,bnhmd->bnhcm', qc, kc) * dmask[None, None]
    intra = jnp.einsum('bnhcm,bnhme->bnhce', scores, vc)

    def step(r, xs):
        qn, kn, vn = xs
        out = jnp.einsum('bhcd,bhde->bhce', qn * xi[None, :, :, None], r)
        r = chunk_decay * r + jnp.einsum('bhcd,bhce->bhde', kn * zeta[None, :, :, None], vn)
        return r, out

    r0 = jnp.zeros((b, h, dk, dv), jnp.float32)
    _, cross = lax.scan(step, r0, (qc.transpose(1, 0, 2, 3, 4),
                                   kc.transpose(1, 0, 2, 3, 4),
                                   vc.transpose(1, 0, 2, 3, 4)))
    y = intra + cross.transpose(1, 0, 2, 3, 4)
    return y.transpose(0, 1, 3, 2, 4).reshape(b, s, h, dv)


def dilated_window_attention(q, k, v, window, dilation):
    b, s, h, dh = q.shape
    r = dilation
    nblk = window // dilation
    unit = r * nblk
    s_pad = -(-s // unit) * unit
    pad = ((0, 0), (0, s_pad - s), (0, 0), (0, 0))
    L = s_pad // r
    nb = L // nblk

    def to_blocks(t):
        t = jnp.pad(t, pad).reshape(b, L, r, h, dh).transpose(0, 2, 3, 1, 4)
        return t.reshape(b, r, h, nb, nblk, dh)

    qb, kb, vb = to_blocks(q), to_blocks(k), to_blocks(v)
    prev = lambda t: jnp.pad(t, ((0, 0), (0, 0), (0, 0), (1, 0), (0, 0), (0, 0)))[:, :, :, :-1]
    kcat = jnp.concatenate([prev(kb), kb], axis=4)
    vcat = jnp.concatenate([prev(vb), vb], axis=4)
    scale = 1.0 / math.sqrt(dh)
    scores = jnp.einsum('brhncd,brhnmd->brhncm', qb, kcat).astype(jnp.float32) * scale
    ci = jnp.arange(nblk)[:, None]
    mi = jnp.arange(2 * nblk)[None, :]
    dist = ci + nblk - mi
    valid = (dist >= 0) & (dist <= nblk)
    first = (jnp.arange(nb) == 0)[:, None, None] & (mi < nblk)[None]
    valid = valid[None] & ~first
    scores = jnp.where(valid, scores, -1e30)
    lse = jax.nn.logsumexp(scores, axis=-1)
    p = jnp.exp(scores - lse[..., None])
    out = jnp.einsum('brhncm,brhnmd->brhncd', p, vcat.astype(jnp.float32))
    out = out.reshape(b, r, h, L, dh).transpose(0, 3, 1, 2, 4).reshape(b, s_pad, h, dh)[:, :s]
    lse = lse.reshape(b, r, h, L).transpose(0, 3, 1, 2).reshape(b, s_pad, h)[:, :s]
    return out, lse


def hybrid_mixer(h, w_in, w_ret_out, w_att_out, w_o):
    b, s, _ = h.shape
    proj = h @ w_in
    idx = list(np.cumsum(SPLIT_SIZES)[:-1])
    rq, rk, rv, rg, aq, ak, av, gr, ga = jnp.split(proj, idx, axis=-1)

    rq = rotary(rq.reshape(b, s, RET_HEADS, RET_DK), RET_DK, RET_THETA)
    rk = rotary(rk.reshape(b, s, RET_HEADS, RET_DK), RET_DK, RET_THETA) * (RET_DK ** -0.5)
    rv = rv.reshape(b, s, RET_HEADS, RET_DV)
    yr = retention_chunkwise(rq, rk, rv)
    mu = jnp.mean(yr, axis=-1, keepdims=True)
    var = jnp.mean(jnp.square(yr - mu), axis=-1, keepdims=True)
    yr = ((yr - mu) * lax.rsqrt(var + NORM_EPS)).reshape(b, s, RET_HEADS * RET_DV)
    y_ret = (jax.nn.silu(rg.astype(jnp.float32)) * yr).astype(h.dtype) @ w_ret_out

    aq = rotary(aq.reshape(b, s, ATT_HEADS, ATT_DH), ATT_ROT_DIM, ATT_THETA)
    ak = rotary(ak.reshape(b, s, ATT_HEADS, ATT_DH), ATT_ROT_DIM, ATT_THETA)
    av = av.reshape(b, s, ATT_HEADS, ATT_DH)
    outs, lses = [], []
    for gi, (window, dilation) in enumerate(ATT_GROUPS):
        sl = slice(gi * ATT_HEADS_PER_GROUP, (gi + 1) * ATT_HEADS_PER_GROUP)
        o, l = dilated_window_attention(aq[:, :, sl], ak[:, :, sl], av[:, :, sl], window, dilation)
        outs.append(o)
        lses.append(l)
    wts = jax.nn.softmax(jnp.stack(lses, axis=0), axis=0)
    ya = jnp.sum(wts[..., None] * jnp.stack(outs, axis=0), axis=0)
    y_att = ya.reshape(b, s, ATT_HEADS_PER_GROUP * ATT_DH).astype(h.dtype) @ w_att_out

    merged = jax.nn.sigmoid(gr) * y_ret + jax.nn.sigmoid(ga) * y_att
    return merged @ w_o


def swiglu(h, w_gate, w_up, w_down):
    return (jax.nn.silu(h @ w_gate) * (h @ w_up)) @ w_down


def setup_inputs(seed: int = 0) -> dict:
    key = jax.random.key(seed)
    ks = jax.random.split(key, 12)
    f32 = jnp.float32

    def w(k, fan_in, fan_out):
        return jax.random.normal(k, (DEPTH, fan_in, fan_out), f32) * (fan_in ** -0.5)

    def gain(k):
        return 1.0 + 0.05 * jax.random.normal(k, (DEPTH, D_MODEL), f32)

    return {
        "x": jax.random.normal(ks[0], (BATCH, SEQ, D_MODEL), f32),
        "w_in": w(ks[1], D_MODEL, D_IN),
        "w_ret_out": w(ks[2], RET_HEADS * RET_DV, D_MODEL),
        "w_att_out": w(ks[3], ATT_HEADS_PER_GROUP * ATT_DH, D_MODEL),
        "w_o": w(ks[4], D_MODEL, D_MODEL),
        "w_ffn_gate": w(ks[5], D_MODEL, D_FF),
        "w_ffn_up": w(ks[6], D_MODEL, D_FF),
        "w_ffn_down": w(ks[7], D_FF, D_MODEL),
        "g_pre_mix": gain(ks[8]),
        "g_post_mix": gain(ks[9]),
        "g_pre_ffn": gain(ks[10]),
        "g_post_ffn": gain(ks[11]),
    }


def reference(x, w_in, w_ret_out, w_att_out, w_o, w_ffn_gate, w_ffn_up, w_ffn_down,
              g_pre_mix, g_post_mix, g_pre_ffn, g_post_ffn):
    for l in range(DEPTH):
        h = rmsnorm(x, g_pre_mix[l])
        m = hybrid_mixer(h, w_in[l], w_ret_out[l], w_att_out[l], w_o[l])
        x = x + rmsnorm(m, g_post_mix[l])
        h = rmsnorm(x, g_pre_ffn[l])
        f = swiglu(h, w_ffn_gate[l], w_ffn_up[l], w_ffn_down[l])
        x = x + rmsnorm(f, g_post_ffn[l])
    return x
```

```python
import functools
import math

import jax
import jax.numpy as jnp
from jax import lax
from jax.experimental import pallas as pl
from jax.experimental.pallas import tpu as pltpu

D_MODEL = 1024
RET_HEADS = 4
RET_DK = 256
RET_DV = 512
RET_THETA = 10000.0
ATT_GROUPS = ((128, 1), (512, 4), (2048, 16))
ATT_HPG = 4
ATT_DH = 128
ATT_ROT_HALF = 16
ATT_THETA = 500000.0
D_FF = 2816
NORM_EPS = 1e-6
NEG = -1e30

LANES = 128
ATT_BLK = 128
TILE = 2048
SUB = 512
COLB = 512
RET_C = 256
VMEM_LIMIT = 60 * 1024 * 1024

_RQ, _RK, _RV, _RG, _AQ, _AK, _AV, _GR, _GA = 0, 2, 4, 8, 12, 15, 18, 21, 23
N_COLB = 25
_ORDER = ([0, 1, 2, 3] + list(range(4, 12)) + [21, 22, 23, 24] + [_AV, _AQ, _AK]
          + [_AQ + 1, _AK + 1, _AV + 1] + [_AQ + 2, _AK + 2, _AV + 2])
_J_PERM4 = 19
_J_PERM16 = 22

_F32 = jnp.float32
_BF16 = jnp.bfloat16


def _dot(a, b):
    return jnp.dot(a, b, preferred_element_type=_F32)


def _dot_nt(a, b):
    return lax.dot_general(a, b, (((1,), (1,)), ((), ())), preferred_element_type=_F32)


def _dot_tn(a, b):
    return lax.dot_general(a, b, (((0,), (0,)), ((), ())), preferred_element_type=_F32)


def _rms(x, g):
    return x * lax.rsqrt(jnp.mean(x * x, axis=-1, keepdims=True) + NORM_EPS) * g


def _in_proj_kernel(order_ref, x_ref, g_ref, w_ref, cr_ref, sr_ref, ca_ref, sa_ref,
                    o_ref, hs_ref, h_ref):
    j = pl.program_id(1)
    n_slab = D_MODEL // LANES
    n_blk = TILE // ATT_BLK

    @pl.when(j == 0)
    def _():
        def body(c, carry):
            rows = pl.ds(pl.multiple_of(c * 256, 256), 256)
            y = _rms(x_ref[rows, :], g_ref[...])
            for k in range(n_slab):
                hs_ref[k, rows, :] = y[:, k * LANES:(k + 1) * LANES]
            h_ref[rows, :] = y.astype(_BF16)
            return carry
        lax.fori_loop(0, TILE // 256, body, 0)

    def permute(r):
        def body(b, carry):
            if r == 16:
                start = b
            else:
                start = (b // r) * (r * ATT_BLK) + (b % r)
            dst = pl.ds(pl.multiple_of(b * ATT_BLK, ATT_BLK), ATT_BLK)
            for k in range(n_slab):
                h_ref[dst, k * LANES:(k + 1) * LANES] = (
                    hs_ref[k, pl.ds(start, ATT_BLK, stride=r), :].astype(_BF16))
            return carry
        lax.fori_loop(0, n_blk, body, 0)

    @pl.when(j == _J_PERM4)
    def _():
        permute(4)

    @pl.when(j == _J_PERM16)
    def _():
        permute(16)

    def table_rows(t_ref, c, bi, r):
        if r == 1:
            return t_ref[pl.ds(pl.multiple_of(c * SUB + bi * ATT_BLK, ATT_BLK), ATT_BLK), :]
        if r == 4:
            return t_ref[pl.ds(c * SUB + bi, ATT_BLK, stride=4), :]
        return t_ref[pl.ds(c * 4 + bi, ATT_BLK, stride=16), :]

    def run(epilogue):
        def body(c, carry):
            rows = pl.ds(pl.multiple_of(c * SUB, SUB), SUB)
            acc = _dot(h_ref[rows, :], w_ref[...])
            for bi in range(SUB // ATT_BLK):
                a = acc[bi * ATT_BLK:(bi + 1) * ATT_BLK, :]
                dst = pl.ds(pl.multiple_of(c * SUB + bi * ATT_BLK, ATT_BLK), ATT_BLK)
                o_ref[dst, :] = epilogue(a, c, bi).astype(_BF16)
            return carry
        lax.fori_loop(0, TILE // SUB, body, 0)

    def plain(a, c, bi):
        return a

    def ret_rotary(scale):
        def f(a, c, bi):
            cos = table_rows(cr_ref, c, bi, 1)
            sin = table_rows(sr_ref, c, bi, 1)
            outs = []
            for hh in range(COLB // RET_DK):
                x1 = a[:, hh * RET_DK:hh * RET_DK + LANES]
                x2 = a[:, hh * RET_DK + LANES:(hh + 1) * RET_DK]
                outs.append((x1 * cos - x2 * sin) * scale)
                outs.append((x1 * sin + x2 * cos) * scale)
            return jnp.concatenate(outs, axis=-1)
        return f

    def att_rotary(r):
        def f(a, c, bi):
            cos = table_rows(ca_ref, c, bi, r)
            sin = table_rows(sa_ref, c, bi, r)
            lane = lax.broadcasted_iota(jnp.int32, (ATT_BLK, LANES), 1)
            outs = []
            for hh in range(ATT_HPG):
                x = a[:, hh * ATT_DH:(hh + 1) * ATT_DH]
                partner = jnp.where(lane < ATT_ROT_HALF,
                                    pltpu.roll(x, LANES - ATT_ROT_HALF, 1),
                                    pltpu.roll(x, ATT_ROT_HALF, 1))
                outs.append(x * cos + partner * sin)
            return jnp.concatenate(outs, axis=-1)
        return f

    @pl.when(j < 2)
    def _():
        run(ret_rotary(1.0))

    @pl.when((j >= 2) & (j < 4))
    def _():
        run(ret_rotary(RET_DK ** -0.5))

    @pl.when(((j >= 4) & (j < 17)) | (j == 21) | (j == 24))
    def _():
        run(plain)

    @pl.when((j == 17) | (j == 18))
    def _():
        run(att_rotary(1))

    @pl.when((j == 19) | (j == 20))
    def _():
        run(att_rotary(4))

    @pl.when((j == 22) | (j == 23))
    def _():
        run(att_rotary(16))


def _in_proj(x, g, w_bf16, tabs, order):
    t = x.shape[0]
    n_tiles = t // TILE
    tiles_per_seq = tabs[0].shape[0] // TILE
    tab_spec = pl.BlockSpec((TILE, LANES), lambda i, j, o: (i % tiles_per_seq, 0))
    grid_spec = pltpu.PrefetchScalarGridSpec(
        num_scalar_prefetch=1,
        grid=(n_tiles, N_COLB),
        in_specs=[
            pl.BlockSpec((TILE, D_MODEL), lambda i, j, o: (i, 0)),
            pl.BlockSpec((1, D_MODEL), lambda i, j, o: (0, 0)),
            pl.BlockSpec((D_MODEL, COLB), lambda i, j, o: (0, o[j])),
            tab_spec, tab_spec, tab_spec, tab_spec,
        ],
        out_specs=pl.BlockSpec((TILE, COLB), lambda i, j, o: (i, o[j])),
        scratch_shapes=[
            pltpu.VMEM((D_MODEL // LANES, TILE, LANES), _F32),
            pltpu.VMEM((TILE, D_MODEL), _BF16),
        ],
    )
    return pl.pallas_call(
        _in_proj_kernel,
        grid_spec=grid_spec,
        out_shape=jax.ShapeDtypeStruct((t, N_COLB * COLB), _BF16),
        compiler_params=pltpu.CompilerParams(
            dimension_semantics=("arbitrary", "arbitrary"),
            vmem_limit_bytes=VMEM_LIMIT),
        name="in_proj",
    )(order, x, g, w_bf16, *tabs)


def _retention_kernel(q_ref, k_ref, v_ref, g_ref, dm_ref, xi_ref, ze_ref, dec_ref,
                      o_ref, r_ref, *, n_chunks):
    @pl.when(pl.program_id(1) == 0)
    def _():
        r_ref[...] = jnp.zeros_like(r_ref)

    for c in range(n_chunks):
        rows = slice(c * RET_C, (c + 1) * RET_C)
        for h in range(RET_HEADS):
            q = q_ref[rows, h * RET_DK:(h + 1) * RET_DK]
            k = k_ref[rows, h * RET_DK:(h + 1) * RET_DK]
            v = v_ref[rows, h * RET_DV:(h + 1) * RET_DV]
            scores = _dot_nt(q, k) * dm_ref[h]
            intra = _dot(scores.astype(_BF16), v)
            qx = (q.astype(_F32) * xi_ref[h]).astype(_BF16)
            r_old = r_ref[h]
            cross = _dot(qx, r_old.astype(_BF16))
            kz = (k.astype(_F32) * ze_ref[h]).astype(_BF16)
            r_ref[h] = dec_ref[h] * r_old + _dot_tn(kz, v)
            y = intra + cross
            mu = jnp.mean(y, axis=-1, keepdims=True)
            yc = y - mu
            var = jnp.mean(yc * yc, axis=-1, keepdims=True)
            yn = yc * lax.rsqrt(var + NORM_EPS)
            gate = g_ref[rows, h * RET_DV:(h + 1) * RET_DV].astype(_F32)
            o_ref[rows, h * RET_DV:(h + 1) * RET_DV] = (
                gate * jax.nn.sigmoid(gate) * yn).astype(_BF16)


def _retention(proj, consts, batch, seq, tc=512):
    steps = seq // tc
    dm, xi, ze, dec = consts
    qk_w = RET_HEADS * RET_DK
    v_w = RET_HEADS * RET_DV

    def const_spec(a):
        return pl.BlockSpec(a.shape, lambda b, s: (0,) * a.ndim)

    return pl.pallas_call(
        functools.partial(_retention_kernel, n_chunks=tc // RET_C),
        grid=(batch, steps),
        in_specs=[
            pl.BlockSpec((tc, qk_w), lambda b, s: (b * steps + s, _RQ * COLB // qk_w)),
            pl.BlockSpec((tc, qk_w), lambda b, s: (b * steps + s, _RK * COLB // qk_w)),
            pl.BlockSpec((tc, v_w), lambda b, s: (b * steps + s, _RV * COLB // v_w)),
            pl.BlockSpec((tc, v_w), lambda b, s: (b * steps + s, _RG * COLB // v_w)),
            const_spec(dm), const_spec(xi), const_spec(ze), const_spec(dec),
        ],
        out_specs=pl.BlockSpec((tc, v_w), lambda b, s: (b * steps + s, 0)),
        out_shape=jax.ShapeDtypeStruct((batch * seq, v_w), _BF16),
        scratch_shapes=[pltpu.VMEM((RET_HEADS, RET_DK, RET_DV), _F32)],
        compiler_params=pltpu.CompilerParams(
            dimension_semantics=("arbitrary", "arbitrary"),
            vmem_limit_bytes=VMEM_LIMIT),
        name="retention",
    )(proj, proj, proj, proj, dm, xi, ze, dec)


def _dilated_kernel(q0, k0, v0, kp0, vp0, q1, k1, v1, kp1, vp1, q2, k2, v2, kp2, vp2,
                    o_ref, num_ref, st_ref):
    i = pl.program_id(1)
    ss = pl.program_id(2)
    n_sub = TILE // SUB
    scale = 1.0 / math.sqrt(ATT_DH)
    ci = lax.broadcasted_iota(jnp.int32, (ATT_BLK, ATT_BLK), 0)
    mi = lax.broadcasted_iota(jnp.int32, (ATT_BLK, ATT_BLK), 1)
    prev_band = mi >= ci
    cur_band = mi <= ci
    lane = lax.broadcasted_iota(jnp.int32, (ATT_BLK, LANES), 1)

    groups = (
        (q0, k0, v0, kp0, vp0, 1),
        (q1, k1, v1, kp1, vp1, 4),
        (q2, k2, v2, kp2, vp2, 16),
    )
    for g, (q_ref, k_ref, v_ref, kp_ref, vp_ref, r) in enumerate(groups):
        for bi in range(SUB // ATT_BLK):
            rows = slice(bi * ATT_BLK, (bi + 1) * ATT_BLK)
            if r == 1:
                if bi == 0:
                    pk_ref, pv_ref, prows = kp_ref, vp_ref, slice(0, ATT_BLK)
                    has_prev = (i > 0) | (ss > 0)
                else:
                    pk_ref, pv_ref = k_ref, v_ref
                    prows = slice((bi - 1) * ATT_BLK, bi * ATT_BLK)
                    has_prev = None
                dst = pl.ds(pl.multiple_of(ss * SUB + bi * ATT_BLK, ATT_BLK), ATT_BLK)
            elif r == 4:
                pk_ref, pv_ref, prows = kp_ref, vp_ref, rows
                has_prev = (i > 0) | (ss > 0)
                dst = pl.ds(ss * SUB + bi, ATT_BLK, stride=4)
            else:
                pk_ref, pv_ref, prows = kp_ref, vp_ref, rows
                has_prev = i > 0
                dst = pl.ds(ss * 4 + bi, ATT_BLK, stride=16)
            pmask = prev_band if has_prev is None else (prev_band & has_prev)
            stats = jnp.zeros((ATT_BLK, LANES), _F32)
            for h in range(ATT_HPG):
                cols = slice(h * ATT_DH, (h + 1) * ATT_DH)
                q = q_ref[rows, cols]
                sp = jnp.where(pmask, _dot_nt(q, pk_ref[prows, cols]) * scale, NEG)
                sc = jnp.where(cur_band, _dot_nt(q, k_ref[rows, cols]) * scale, NEG)
                m = jnp.maximum(jnp.max(sp, axis=-1, keepdims=True),
                                jnp.max(sc, axis=-1, keepdims=True))
                pp = jnp.exp(sp - m)
                pc = jnp.exp(sc - m)
                l = jnp.sum(pp, axis=-1, keepdims=True) + jnp.sum(pc, axis=-1, keepdims=True)
                num = (_dot(pp.astype(_BF16), pv_ref[prows, cols])
                       + _dot(pc.astype(_BF16), v_ref[rows, cols]))
                num_ref[g, h, dst, :] = num
                stats = jnp.where(lane == h, m, stats)
                stats = jnp.where(lane == ATT_HPG + h, l, stats)
            st_ref[g, dst, :] = stats

    @pl.when(ss == n_sub - 1)
    def _():
        n_g = len(ATT_GROUPS)

        def body(c, carry):
            rows = pl.ds(pl.multiple_of(c * ATT_BLK, ATT_BLK), ATT_BLK)
            st = [st_ref[g, rows, :] for g in range(n_g)]
            outs = []
            for h in range(ATT_HPG):
                ms = [st[g][:, h:h + 1] for g in range(n_g)]
                ls = [st[g][:, ATT_HPG + h:ATT_HPG + h + 1] for g in range(n_g)]
                top = functools.reduce(jnp.maximum, ms)
                ws = [jnp.exp(m - top) for m in ms]
                den = sum(w * l for w, l in zip(ws, ls))
                inv = 1.0 / den
                outs.append(sum((ws[g] * inv) * num_ref[g, h, rows, :] for g in range(n_g)))
            o_ref[rows, :] = jnp.concatenate(outs, axis=-1).astype(_BF16)
            return carry
        lax.fori_loop(0, TILE // ATT_BLK, body, 0)


def _dilated(proj, batch, seq):
    tiles = seq // TILE
    n_sub = TILE // SUB
    subs_per_seq = seq // SUB

    def cur(colb):
        return pl.BlockSpec(
            (SUB, COLB), lambda b, i, s: (b * subs_per_seq + i * n_sub + s, colb))

    def prev_sub(colb, back):
        return pl.BlockSpec(
            (SUB, COLB),
            lambda b, i, s: (b * subs_per_seq + jnp.maximum(i * n_sub + s - back, 0), colb))

    def prev_blk(colb):
        per = SUB // ATT_BLK
        return pl.BlockSpec(
            (ATT_BLK, COLB),
            lambda b, i, s: (b * subs_per_seq * per
                             + jnp.maximum((i * n_sub + s) * per - 1, 0), colb))

    in_specs = []
    for g in range(len(ATT_GROUPS)):
        in_specs += [cur(_AQ + g), cur(_AK + g), cur(_AV + g)]
        if g == 0:
            in_specs += [prev_blk(_AK + g), prev_blk(_AV + g)]
        elif g == 1:
            in_specs += [prev_sub(_AK + g, 1), prev_sub(_AV + g, 1)]
        else:
            in_specs += [prev_sub(_AK + g, n_sub), prev_sub(_AV + g, n_sub)]

    return pl.pallas_call(
        _dilated_kernel,
        grid=(batch, tiles, n_sub),
        in_specs=in_specs,
        out_specs=pl.BlockSpec((TILE, ATT_HPG * ATT_DH), lambda b, i, s: (b * tiles + i, 0)),
        out_shape=jax.ShapeDtypeStruct((batch * seq, ATT_HPG * ATT_DH), _BF16),
        scratch_shapes=[
            pltpu.VMEM((len(ATT_GROUPS), ATT_HPG, TILE, LANES), _F32),
            pltpu.VMEM((len(ATT_GROUPS), TILE, LANES), _F32),
        ],
        compiler_params=pltpu.CompilerParams(
            dimension_semantics=("arbitrary", "arbitrary", "arbitrary"),
            vmem_limit_bytes=VMEM_LIMIT),
        name="dilated",
    )(*([proj] * 15))


def _merge_kernel(x_ref, yr_ref, ya_ref, gr0_ref, gr1_ref, ga0_ref, ga1_ref,
                  wr_ref, wa_ref, wo_ref, g_ref, o_ref):
    y_ret = _dot(yr_ref[...], wr_ref[...])
    y_att = _dot(ya_ref[...], wa_ref[...])
    gr = jnp.concatenate([gr0_ref[...], gr1_ref[...]], axis=-1).astype(_F32)
    ga = jnp.concatenate([ga0_ref[...], ga1_ref[...]], axis=-1).astype(_F32)
    merged = jax.nn.sigmoid(gr) * y_ret + jax.nn.sigmoid(ga) * y_att
    m = _dot(merged.astype(_BF16), wo_ref[...])
    o_ref[...] = x_ref[...] + _rms(m, g_ref[...])


def _resident(shape):
    return pl.BlockSpec(shape, lambda i: (0,) * len(shape), pipeline_mode=pl.Buffered(1))


def _merge(x, yr, ya, proj, wr, wa, wo, g, tm=512):
    t = x.shape[0]
    return pl.pallas_call(
        _merge_kernel,
        grid=(t // tm,),
        in_specs=[
            pl.BlockSpec((tm, D_MODEL), lambda i: (i, 0)),
            pl.BlockSpec((tm, yr.shape[1]), lambda i: (i, 0)),
            pl.BlockSpec((tm, ya.shape[1]), lambda i: (i, 0)),
            pl.BlockSpec((tm, COLB), lambda i: (i, _GR)),
            pl.BlockSpec((tm, COLB), lambda i: (i, _GR + 1)),
            pl.BlockSpec((tm, COLB), lambda i: (i, _GA)),
            pl.BlockSpec((tm, COLB), lambda i: (i, _GA + 1)),
            _resident(wr.shape), _resident(wa.shape), _resident(wo.shape),
            _resident(g.shape),
        ],
        out_specs=pl.BlockSpec((tm, D_MODEL), lambda i: (i, 0)),
        out_shape=jax.ShapeDtypeStruct((t, D_MODEL), _F32),
        compiler_params=pltpu.CompilerParams(
            dimension_semantics=("arbitrary",), vmem_limit_bytes=VMEM_LIMIT),
        name="merge",
    )(x, yr, ya, proj, proj, proj, proj, wr, wa, wo, g)


def _ffn_kernel(x_ref, gpre_ref, wg_ref, wu_ref, wd_ref, gpost_ref, o_ref, a_ref, *, n_split):
    x = x_ref[...]
    h = _rms(x, gpre_ref[...]).astype(_BF16)
    w = D_FF // n_split
    for c in range(n_split):
        cols = slice(c * w, (c + 1) * w)
        gate = _dot(h, wg_ref[:, cols])
        up = _dot(h, wu_ref[:, cols])
        a_ref[:, cols] = (gate * jax.nn.sigmoid(gate) * up).astype(_BF16)
    f = _dot(a_ref[...], wd_ref[...])
    o_ref[...] = x + _rms(f, gpost_ref[...])


def _ffn(x, gpre, wg, wu, wd, gpost, tm=512, n_split=2):
    t = x.shape[0]
    return pl.pallas_call(
        functools.partial(_ffn_kernel, n_split=n_split),
        grid=(t // tm,),
        in_specs=[
            pl.BlockSpec((tm, D_MODEL), lambda i: (i, 0)),
            _resident(gpre.shape), _resident(wg.shape), _resident(wu.shape),
            _resident(wd.shape), _resident(gpost.shape),
        ],
        out_specs=pl.BlockSpec((tm, D_MODEL), lambda i: (i, 0)),
        out_shape=jax.ShapeDtypeStruct((t, D_MODEL), _F32),
        scratch_shapes=[pltpu.VMEM((tm, D_FF), _BF16)],
        compiler_params=pltpu.CompilerParams(
            dimension_semantics=("arbitrary",), vmem_limit_bytes=VMEM_LIMIT),
        name="ffn",
    )(x, gpre, wg, wu, wd, gpost)


def _rotary_tables(seq):
    pos = jnp.arange(seq, dtype=_F32)[:, None]
    half = RET_DK // 2
    ang = pos * (RET_THETA ** (-jnp.arange(half, dtype=_F32) / half))[None, :]
    cos_r, sin_r = jnp.cos(ang), jnp.sin(ang)
    ang = pos * (ATT_THETA ** (-jnp.arange(ATT_ROT_HALF, dtype=_F32) / ATT_ROT_HALF))[None, :]
    cos, sin = jnp.cos(ang), jnp.sin(ang)
    rest = LANES - 2 * ATT_ROT_HALF
    cos_a = jnp.concatenate([cos, cos, jnp.ones((seq, rest), _F32)], axis=-1)
    sin_a = jnp.concatenate([-sin, sin, jnp.zeros((seq, rest), _F32)], axis=-1)
    return cos_r, sin_r, cos_a, sin_a


def _retention_consts():
    c = RET_C
    lg = jnp.log(1.0 - 2.0 ** (-5.0 - jnp.arange(RET_HEADS, dtype=_F32)))
    pos = jnp.arange(c, dtype=_F32)
    rel = pos[:, None] - pos[None, :]
    dmask = jnp.where(rel[None] >= 0,
                      jnp.exp(jnp.maximum(rel, 0.0)[None] * lg[:, None, None]), 0.0)
    xi = jnp.exp((pos + 1.0)[None, :] * lg[:, None])[:, :, None]
    zeta = jnp.exp((c - 1.0 - pos)[None, :] * lg[:, None])[:, :, None]
    decay = jnp.broadcast_to(jnp.exp(c * lg)[:, None, None], (RET_HEADS, 1, RET_DV))
    return dmask, xi, zeta, decay


def kernel(x, w_in, w_ret_out, w_att_out, w_o, w_ffn_gate, w_ffn_up, w_ffn_down,
           g_pre_mix, g_post_mix, g_pre_ffn, g_post_ffn):
    batch, seq, d = x.shape
    depth = w_in.shape[0]
    assert d == D_MODEL and seq % TILE == 0
    tabs = _rotary_tables(seq)
    consts = _retention_consts()
    order = jnp.asarray(_ORDER, jnp.int32)
    xf = x.reshape(batch * seq, d)
    for l in range(depth):
        proj = _in_proj(xf, g_pre_mix[l][None, :], w_in[l].astype(_BF16), tabs, order)
        yr = _retention(proj, consts, batch, seq)
        ya = _dilated(proj, batch, seq)
        xf = _merge(xf, yr, ya, proj,
                    w_ret_out[l].astype(_BF16), w_att_out[l].astype(_BF16),
                    w_o[l].astype(_BF16), g_post_mix[l][None, :])
        xf = _ffn(xf, g_pre_ffn[l][None, :], w_ffn_gate[l].astype(_BF16),
                  w_ffn_up[l].astype(_BF16), w_ffn_down[l].astype(_BF16),
                  g_post_ffn[l][None, :])
    return xf.reshape(batch, seq, d)
```

```python
import functools
import math

import jax
import jax.numpy as jnp
from jax import lax
from jax.experimental import pallas as pl
from jax.experimental.pallas import tpu as pltpu

D_MODEL = 1024
RET_HEADS = 4
RET_DK = 256
RET_DV = 512
RET_THETA = 10000.0
ATT_GROUPS = ((128, 1), (512, 4), (2048, 16))
ATT_HPG = 4
ATT_DH = 128
ATT_ROT_HALF = 16
ATT_THETA = 500000.0
D_FF = 2816
NORM_EPS = 1e-6
NEG = -1e30

LANES = 128
MXU_W = 256
ATT_BLK = 128
TILE = 2048
SUB = 512
RET_C = 256
VMEM_LIMIT = 60 * 1024 * 1024

MAIN_W = 8192
MAIN_COLB = 1024
_RQ, _RK, _RV, _RG, _GR, _GA = 0, 1, 2, 4, 6, 7
ATT_GW = ATT_HPG * ATT_DH
ATT_COL_LO = 2 * RET_HEADS * RET_DK + 2 * RET_HEADS * RET_DV
GATE_COL_LO = ATT_COL_LO + 3 * len(ATT_GROUPS) * ATT_GW

_F32 = jnp.float32
_BF16 = jnp.bfloat16


def _dot(a, b):
    return jnp.dot(a, b, preferred_element_type=_F32)


def _dot_nt(a, b):
    return lax.dot_general(a, b, (((1,), (1,)), ((), ())), preferred_element_type=_F32)


def _dot_tn(a, b):
    return lax.dot_general(a, b, (((0,), (0,)), ((), ())), preferred_element_type=_F32)


def _rms(x, g):
    return x * lax.rsqrt(jnp.mean(x * x, axis=-1, keepdims=True) + NORM_EPS) * g


def _proj_main_kernel(x_ref, g_ref, w_ref, wlo_ref, whi_ref, cr_ref, sr_ref, o_ref, h_ref):
    j = pl.program_id(1)

    @pl.when(j == 0)
    def _():
        def body(c, carry):
            rows = pl.ds(pl.multiple_of(c * 256, 256), 256)
            h_ref[rows, :] = _rms(x_ref[rows, :], g_ref[...]).astype(_BF16)
            return carry
        lax.fori_loop(0, TILE // 256, body, 0)

    def run(epilogue):
        for c in range(TILE // SUB):
            epilogue(_dot(h_ref[c * SUB:(c + 1) * SUB, :], w_ref[...]), c)

    def plain(acc, c):
        o_ref[c * SUB:(c + 1) * SUB, :] = acc.astype(_BF16)

    def run_halves():
        half = MAIN_COLB // 2
        for c in range(TILE // SUB):
            lhs = h_ref[c * SUB:(c + 1) * SUB, :]
            o_ref[c * SUB:(c + 1) * SUB, :half] = _dot(lhs, wlo_ref[...]).astype(_BF16)
            o_ref[c * SUB:(c + 1) * SUB, half:] = _dot(lhs, whi_ref[...]).astype(_BF16)

    def ret_rotary(scale):
        def f(acc, c):
            for bi in range(SUB // ATT_BLK):
                lo = c * SUB + bi * ATT_BLK
                cos = cr_ref[lo:lo + ATT_BLK, :]
                sin = sr_ref[lo:lo + ATT_BLK, :]
                a = acc[bi * ATT_BLK:(bi + 1) * ATT_BLK, :]
                outs = []
                for hh in range(MAIN_COLB // RET_DK):
                    x1 = a[:, hh * RET_DK:hh * RET_DK + LANES]
                    x2 = a[:, hh * RET_DK + LANES:(hh + 1) * RET_DK]
                    outs.append((x1 * cos - x2 * sin) * scale)
                    outs.append((x1 * sin + x2 * cos) * scale)
                o_ref[lo:lo + ATT_BLK, :] = jnp.concatenate(outs, axis=-1).astype(_BF16)
        return f

    @pl.when(j == _RQ)
    def _():
        run(ret_rotary(1.0))

    @pl.when(j == _RK)
    def _():
        run(ret_rotary(RET_DK ** -0.5))

    @pl.when((j >= _RV) & (j < _GR))
    def _():
        run(plain)

    @pl.when(j >= _GR)
    def _():
        run_halves()


def _proj_main(x, g, w_in, cos_r, sin_r):
    t = x.shape[0]
    tiles_per_seq = cos_r.shape[0] // TILE
    tab_spec = pl.BlockSpec((TILE, LANES), lambda i, j: (i % tiles_per_seq, 0))
    half = MAIN_COLB // 2
    gate_blk = GATE_COL_LO // half

    def gate_half(which):
        return pl.BlockSpec(
            (D_MODEL, half),
            lambda i, j: (0, gate_blk + 2 * jnp.maximum(j - _GR, 0) + which))

    return pl.pallas_call(
        _proj_main_kernel,
        grid=(t // TILE, MAIN_W // MAIN_COLB),
        in_specs=[
            pl.BlockSpec((TILE, D_MODEL), lambda i, j: (i, 0)),
            pl.BlockSpec((1, D_MODEL), lambda i, j: (0, 0)),
            pl.BlockSpec((D_MODEL, MAIN_COLB), lambda i, j: (0, jnp.minimum(j, _GR - 1))),
            gate_half(0), gate_half(1),
            tab_spec, tab_spec,
        ],
        out_specs=[
            pl.BlockSpec((TILE, MAIN_COLB), lambda i, j: (i, j)),
            pl.BlockSpec((TILE, D_MODEL), lambda i, j: (i, 0)),
        ],
        out_shape=[
            jax.ShapeDtypeStruct((t, MAIN_W), _BF16),
            jax.ShapeDtypeStruct((t, D_MODEL), _BF16),
        ],
        compiler_params=pltpu.CompilerParams(
            dimension_semantics=("arbitrary", "arbitrary"),
            vmem_limit_bytes=VMEM_LIMIT),
        name="proj_main",
    )(x, g, w_in, w_in, w_in, cos_r, sin_r)


def _proj_att_kernel(h_ref, wq_ref, wk_ref, wv_ref, ca_ref, sa_ref, o_ref, hs_ref, hp_ref):
    g = pl.program_id(1)
    n_slab = D_MODEL // LANES
    n_blk = TILE // ATT_BLK

    @pl.when(g == 0)
    def _():
        def body(c, carry):
            rows = pl.ds(pl.multiple_of(c * 256, 256), 256)
            y = h_ref[rows, :].astype(_F32)
            for k in range(n_slab):
                hs_ref[k, rows, :] = y[:, k * LANES:(k + 1) * LANES]
            return carry
        lax.fori_loop(0, TILE // 256, body, 0)

    def permute(r):
        def body(b, carry):
            if r == 16:
                start = b
            else:
                start = (b // r) * (r * ATT_BLK) + (b % r)
            dst = pl.ds(pl.multiple_of(b * ATT_BLK, ATT_BLK), ATT_BLK)
            for k in range(n_slab):
                hp_ref[dst, k * LANES:(k + 1) * LANES] = (
                    hs_ref[k, pl.ds(start, ATT_BLK, stride=r), :].astype(_BF16))
            return carry
        lax.fori_loop(0, n_blk, body, 0, unroll=2)

    def table_rows(t_ref, c, bi, r):
        if r == 1:
            return t_ref[c * SUB + bi * ATT_BLK:c * SUB + (bi + 1) * ATT_BLK, :]
        if r == 4:
            return t_ref[pl.ds(c * SUB + bi, ATT_BLK, stride=4), :]
        return t_ref[pl.ds(c * 4 + bi, ATT_BLK, stride=16), :]

    lane = lax.broadcasted_iota(jnp.int32, (ATT_BLK, LANES), 1)

    def run(lhs_ref, r):
        for c in range(TILE // SUB):
            lhs = lhs_ref[c * SUB:(c + 1) * SUB, :]
            o_ref[2, c * SUB:(c + 1) * SUB, :] = _dot(lhs, wv_ref[...]).astype(_BF16)
            for t, w_ref in enumerate((wq_ref, wk_ref)):
                acc = _dot(lhs, w_ref[...])
                for bi in range(SUB // ATT_BLK):
                    lo = c * SUB + bi * ATT_BLK
                    cos = table_rows(ca_ref, c, bi, r)
                    sin = table_rows(sa_ref, c, bi, r)
                    outs = []
                    for hh in range(ATT_HPG):
                        x = acc[bi * ATT_BLK:(bi + 1) * ATT_BLK, hh * ATT_DH:(hh + 1) * ATT_DH]
                        partner = jnp.where(lane < ATT_ROT_HALF,
                                            pltpu.roll(x, LANES - ATT_ROT_HALF, 1),
                                            pltpu.roll(x, ATT_ROT_HALF, 1))
                        outs.append(x * cos + partner * sin)
                    o_ref[t, lo:lo + ATT_BLK, :] = jnp.concatenate(outs, axis=-1).astype(_BF16)

    @pl.when(g == 0)
    def _():
        run(h_ref, 1)

    @pl.when(g == 1)
    def _():
        permute(4)
        run(hp_ref, 4)

    @pl.when(g == 2)
    def _():
        permute(16)
        run(hp_ref, 16)


def _proj_att(h, w_in, cos_a, sin_a):
    t = h.shape[0]
    n_g = len(ATT_GROUPS)
    tiles_per_seq = cos_a.shape[0] // TILE
    tab_spec = pl.BlockSpec((TILE, LANES), lambda i, g: (i % tiles_per_seq, 0))
    att_blk = ATT_COL_LO // ATT_GW

    def w_spec(which):
        return pl.BlockSpec((D_MODEL, ATT_GW), lambda i, g: (0, att_blk + which * n_g + g))

    return pl.pallas_call(
        _proj_att_kernel,
        grid=(t // TILE, n_g),
        in_specs=[
            pl.BlockSpec((TILE, D_MODEL), lambda i, g: (i, 0)),
            w_spec(0), w_spec(1), w_spec(2),
            tab_spec, tab_spec,
        ],
        out_specs=pl.BlockSpec((None, 3, TILE, ATT_GW), lambda i, g: (g, 0, i, 0)),
        out_shape=jax.ShapeDtypeStruct((n_g, 3, t, ATT_GW), _BF16),
        scratch_shapes=[
            pltpu.VMEM((D_MODEL // LANES, TILE, LANES), _F32),
            pltpu.VMEM((TILE, D_MODEL), _BF16),
        ],
        compiler_params=pltpu.CompilerParams(
            dimension_semantics=("arbitrary", "arbitrary"),
            vmem_limit_bytes=VMEM_LIMIT),
        name="proj_att",
    )(h, w_in, w_in, w_in, cos_a, sin_a)


def _retention_kernel(q_ref, k_ref, v_ref, g_ref, dm_ref, xi_ref, ze_ref, dec_ref,
                      o_ref, r_ref, *, n_chunks):
    @pl.when(pl.program_id(1) == 0)
    def _():
        r_ref[...] = jnp.zeros_like(r_ref)

    for c in range(n_chunks):
        rows = slice(c * RET_C, (c + 1) * RET_C)
        for h in range(RET_HEADS):
            q = q_ref[rows, h * RET_DK:(h + 1) * RET_DK]
            k = k_ref[rows, h * RET_DK:(h + 1) * RET_DK]
            v = v_ref[rows, h * RET_DV:(h + 1) * RET_DV]
            scores = _dot_nt(q, k) * dm_ref[h]
            qx = (q.astype(_F32) * xi_ref[h]).astype(_BF16)
            r_old = r_ref[h]
            y = _dot(jnp.concatenate([scores.astype(_BF16), qx], axis=1),
                     jnp.concatenate([v, r_old.astype(_BF16)], axis=0))
            kz = (k.astype(_F32) * ze_ref[h]).astype(_BF16)
            r_ref[h] = dec_ref[h] * r_old + _dot_tn(kz, v)
            mu = jnp.mean(y, axis=-1, keepdims=True)
            yc = y - mu
            var = jnp.mean(yc * yc, axis=-1, keepdims=True)
            yn = yc * lax.rsqrt(var + NORM_EPS)
            gate = g_ref[rows, h * RET_DV:(h + 1) * RET_DV].astype(_F32)
            o_ref[rows, h * RET_DV:(h + 1) * RET_DV] = (
                gate * jax.nn.sigmoid(gate) * yn).astype(_BF16)


def _retention(proj, consts, batch, seq, tc=512):
    steps = seq // tc
    dm, xi, ze, dec = consts
    qk_w = RET_HEADS * RET_DK
    v_w = RET_HEADS * RET_DV
    colb = MAIN_COLB

    def const_spec(a):
        return pl.BlockSpec(a.shape, lambda b, s: (0,) * a.ndim)

    return pl.pallas_call(
        functools.partial(_retention_kernel, n_chunks=tc // RET_C),
        grid=(batch, steps),
        in_specs=[
            pl.BlockSpec((tc, qk_w), lambda b, s: (b * steps + s, _RQ * colb // qk_w)),
            pl.BlockSpec((tc, qk_w), lambda b, s: (b * steps + s, _RK * colb // qk_w)),
            pl.BlockSpec((tc, v_w), lambda b, s: (b * steps + s, _RV * colb // v_w)),
            pl.BlockSpec((tc, v_w), lambda b, s: (b * steps + s, _RG * colb // v_w)),
            const_spec(dm), const_spec(xi), const_spec(ze), const_spec(dec),
        ],
        out_specs=pl.BlockSpec((tc, v_w), lambda b, s: (b * steps + s, 0)),
        out_shape=jax.ShapeDtypeStruct((batch * seq, v_w), _BF16),
        scratch_shapes=[pltpu.VMEM((RET_HEADS, RET_DK, RET_DV), _F32)],
        compiler_params=pltpu.CompilerParams(
            dimension_semantics=("arbitrary", "arbitrary"),
            vmem_limit_bytes=VMEM_LIMIT),
        name="retention",
    )(proj, proj, proj, proj, dm, xi, ze, dec)


def _dilated_kernel(q0, k0, v0, kp0, vp0, q1, k1, v1, kp1, vp1, q2, k2, v2, kp2, vp2,
                    o_ref, out_ref, lse_ref):
    i = pl.program_id(1)
    ss = pl.program_id(2)
    n_sub = TILE // SUB
    scale = 1.0 / math.sqrt(ATT_DH)
    ci = lax.broadcasted_iota(jnp.int32, (ATT_BLK, ATT_BLK), 0)
    mi = lax.broadcasted_iota(jnp.int32, (ATT_BLK, ATT_BLK), 1)
    prev_band = mi >= ci
    cur_band = mi <= ci

    groups = (
        (q0, k0, v0, kp0, vp0, 1),
        (q1, k1, v1, kp1, vp1, 4),
        (q2, k2, v2, kp2, vp2, 16),
    )

    def item(g, bi):
        q_ref, k_ref, v_ref, kp_ref, vp_ref, r = groups[g]
        rows = slice(bi * ATT_BLK, (bi + 1) * ATT_BLK)
        if r == 1:
            if bi == 0:
                pk_ref, pv_ref, prows = kp_ref, vp_ref, slice(0, ATT_BLK)
                has_prev = (i > 0) | (ss > 0)
            else:
                pk_ref, pv_ref = k_ref, v_ref
                prows = slice((bi - 1) * ATT_BLK, bi * ATT_BLK)
                has_prev = None
            dst = pl.ds(pl.multiple_of(ss * SUB + bi * ATT_BLK, ATT_BLK), ATT_BLK)
        elif r == 4:
            pk_ref, pv_ref, prows = kp_ref, vp_ref, rows
            has_prev = (i > 0) | (ss > 0)
            dst = pl.ds(ss * SUB + bi, ATT_BLK, stride=4)
        else:
            pk_ref, pv_ref, prows = kp_ref, vp_ref, rows
            has_prev = i > 0
            dst = pl.ds(ss * 4 + bi, ATT_BLK, stride=16)
        pmask = prev_band if has_prev is None else (prev_band & has_prev)
        return dict(g=g, q_ref=q_ref, k_ref=k_ref, v_ref=v_ref, pk_ref=pk_ref, pv_ref=pv_ref,
                    rows=rows, prows=prows, pmask=pmask, dst=dst)

    def head_cols(h):
        return slice(h * ATT_DH, (h + 1) * ATT_DH)

    def score_stage(it):
        out = []
        for h in range(ATT_HPG):
            q = it["q_ref"][it["rows"], head_cols(h)]
            out.append((_dot_nt(q, it["pk_ref"][it["prows"], head_cols(h)]),
                        _dot_nt(q, it["k_ref"][it["rows"], head_cols(h)])))
        return out

    def softmax_stage(it, scores):
        out = []
        for sp, sc in scores:
            sp = jnp.where(it["pmask"], sp * scale, NEG)
            sc = jnp.where(cur_band, sc * scale, NEG)
            m = jnp.max(jnp.maximum(sp, sc), axis=-1, keepdims=True)
            pp = jnp.exp(sp - m)
            pc = jnp.exp(sc - m)
            l = jnp.sum(pp + pc, axis=-1, keepdims=True)
            out.append((pp.astype(_BF16), pc.astype(_BF16), m, l))
        return out

    def value_stage(it, probs):
        for h, (pp, pc, m, l) in enumerate(probs):
            num = (_dot(pp, it["pv_ref"][it["prows"], head_cols(h)])
                   + _dot(pc, it["v_ref"][it["rows"], head_cols(h)]))
            out_ref[it["g"], h, it["dst"], :] = num * (1.0 / l)
            lse_ref[it["g"], h, it["dst"], :] = jnp.broadcast_to(m + jnp.log(l),
                                                                 (ATT_BLK, LANES))

    items = [item(g, bi) for g in range(len(groups)) for bi in range(SUB // ATT_BLK)]
    scores = score_stage(items[0])
    for n, it in enumerate(items):
        nxt = score_stage(items[n + 1]) if n + 1 < len(items) else None
        value_stage(it, softmax_stage(it, scores))
        scores = nxt

    @pl.when(ss == n_sub - 1)
    def _():
        n_g = len(ATT_GROUPS)

        def body(c, carry):
            rows = pl.ds(pl.multiple_of(c * ATT_BLK, ATT_BLK), ATT_BLK)
            outs = []
            for h in range(ATT_HPG):
                lses = [lse_ref[g, h, rows, :] for g in range(n_g)]
                top = functools.reduce(jnp.maximum, lses)
                ws = [jnp.exp(x - top) for x in lses]
                inv = 1.0 / sum(ws)
                outs.append(sum(ws[g] * out_ref[g, h, rows, :] for g in range(n_g)) * inv)
            o_ref[rows, :] = jnp.concatenate(outs, axis=-1).astype(_BF16)
            return carry
        lax.fori_loop(0, TILE // ATT_BLK, body, 0)


def _dilated(att, batch, seq):
    tiles = seq // TILE
    n_sub = TILE // SUB
    subs_per_seq = seq // SUB
    per = SUB // ATT_BLK

    def cur(g, t):
        return pl.BlockSpec(
            (None, None, SUB, ATT_GW),
            lambda b, i, s: (g, t, b * subs_per_seq + i * n_sub + s, 0))

    def prev_sub(g, t, back):
        return pl.BlockSpec(
            (None, None, SUB, ATT_GW),
            lambda b, i, s: (g, t, b * subs_per_seq + jnp.maximum(i * n_sub + s - back, 0), 0))

    def prev_blk(g, t):
        return pl.BlockSpec(
            (None, None, ATT_BLK, ATT_GW),
            lambda b, i, s: (g, t, b * subs_per_seq * per
                             + jnp.maximum((i * n_sub + s) * per - 1, 0), 0))

    in_specs = []
    for g in range(len(ATT_GROUPS)):
        in_specs += [cur(g, 0), cur(g, 1), cur(g, 2)]
        if g == 0:
            in_specs += [prev_blk(g, 1), prev_blk(g, 2)]
        elif g == 1:
            in_specs += [prev_sub(g, 1, 1), prev_sub(g, 2, 1)]
        else:
            in_specs += [prev_sub(g, 1, n_sub), prev_sub(g, 2, n_sub)]

    return pl.pallas_call(
        _dilated_kernel,
        grid=(batch, tiles, n_sub),
        in_specs=in_specs,
        out_specs=pl.BlockSpec((TILE, ATT_GW), lambda b, i, s: (b * tiles + i, 0)),
        out_shape=jax.ShapeDtypeStruct((batch * seq, ATT_GW), _BF16),
        scratch_shapes=[
            pltpu.VMEM((len(ATT_GROUPS), ATT_HPG, TILE, LANES), _F32),
            pltpu.VMEM((len(ATT_GROUPS), ATT_HPG, TILE, LANES), _F32),
        ],
        compiler_params=pltpu.CompilerParams(
            dimension_semantics=("arbitrary", "arbitrary", "arbitrary"),
            vmem_limit_bytes=VMEM_LIMIT),
        name="dilated",
    )(*([att] * 15))


def _merge_kernel(x_ref, yr_ref, ya_ref, gr_ref, ga_ref, wr_ref, wa_ref, wo_ref, g_ref, o_ref):
    y_ret = _dot(yr_ref[...], wr_ref[...])
    y_att = _dot(ya_ref[...], wa_ref[...])
    merged = (jax.nn.sigmoid(gr_ref[...].astype(_F32)) * y_ret
              + jax.nn.sigmoid(ga_ref[...].astype(_F32)) * y_att)
    m = _dot(merged.astype(_BF16), wo_ref[...])
    o_ref[...] = x_ref[...] + _rms(m, g_ref[...])


def _resident(shape):
    return pl.BlockSpec(shape, lambda i: (0,) * len(shape), pipeline_mode=pl.Buffered(1))


def _merge(x, yr, ya, proj, wr, wa, wo, g, tm=512):
    t = x.shape[0]
    return pl.pallas_call(
        _merge_kernel,
        grid=(t // tm,),
        in_specs=[
            pl.BlockSpec((tm, D_MODEL), lambda i: (i, 0)),
            pl.BlockSpec((tm, yr.shape[1]), lambda i: (i, 0)),
            pl.BlockSpec((tm, ya.shape[1]), lambda i: (i, 0)),
            pl.BlockSpec((tm, D_MODEL), lambda i: (i, _GR * MAIN_COLB // D_MODEL)),
            pl.BlockSpec((tm, D_MODEL), lambda i: (i, _GA * MAIN_COLB // D_MODEL)),
            _resident(wr.shape), _resident(wa.shape), _resident(wo.shape),
            _resident(g.shape),
        ],
        out_specs=pl.BlockSpec((tm, D_MODEL), lambda i: (i, 0)),
        out_shape=jax.ShapeDtypeStruct((t, D_MODEL), _F32),
        compiler_params=pltpu.CompilerParams(
            dimension_semantics=("arbitrary",), vmem_limit_bytes=VMEM_LIMIT),
        name="merge",
    )(x, yr, ya, proj, proj, wr, wa, wo, g)


def _ffn_kernel(x_ref, gpre_ref, wg_ref, wu_ref, wd_ref, gpost_ref, o_ref, a_ref, *, bounds):
    x = x_ref[...]
    h = _rms(x, gpre_ref[...]).astype(_BF16)
    for lo, hi in zip(bounds[:-1], bounds[1:]):
        cols = slice(lo, hi)
        gate = _dot(h, wg_ref[:, cols])
        up = _dot(h, wu_ref[:, cols])
        a_ref[:, cols] = (gate * jax.nn.sigmoid(gate) * up).astype(_BF16)
    f = _dot(a_ref[...], wd_ref[...])
    o_ref[...] = x + _rms(f, gpost_ref[...])


def _ffn(x, gpre, wg, wu, wd, gpost, tm=512):
    t = x.shape[0]
    bounds = (0, 6 * MXU_W, D_FF)
    return pl.pallas_call(
        functools.partial(_ffn_kernel, bounds=bounds),
        grid=(t // tm,),
        in_specs=[
            pl.BlockSpec((tm, D_MODEL), lambda i: (i, 0)),
            _resident(gpre.shape), _resident(wg.shape), _resident(wu.shape),
            _resident(wd.shape), _resident(gpost.shape),
        ],
        out_specs=pl.BlockSpec((tm, D_MODEL), lambda i: (i, 0)),
        out_shape=jax.ShapeDtypeStruct((t, D_MODEL), _F32),
        scratch_shapes=[pltpu.VMEM((tm, D_FF), _BF16)],
        compiler_params=pltpu.CompilerParams(
            dimension_semantics=("arbitrary",), vmem_limit_bytes=VMEM_LIMIT),
        name="ffn",
    )(x, gpre, wg, wu, wd, gpost)


def _rotary_tables(seq):
    pos = jnp.arange(seq, dtype=_F32)[:, None]
    half = RET_DK // 2
    ang = pos * (RET_THETA ** (-jnp.arange(half, dtype=_F32) / half))[None, :]
    cos_r, sin_r = jnp.cos(ang), jnp.sin(ang)
    ang = pos * (ATT_THETA ** (-jnp.arange(ATT_ROT_HALF, dtype=_F32) / ATT_ROT_HALF))[None, :]
    cos, sin = jnp.cos(ang), jnp.sin(ang)
    rest = LANES - 2 * ATT_ROT_HALF
    cos_a = jnp.concatenate([cos, cos, jnp.ones((seq, rest), _F32)], axis=-1)
    sin_a = jnp.concatenate([-sin, sin, jnp.zeros((seq, rest), _F32)], axis=-1)
    return cos_r, sin_r, cos_a, sin_a


def _retention_consts():
    c = RET_C
    lg = jnp.log(1.0 - 2.0 ** (-5.0 - jnp.arange(RET_HEADS, dtype=_F32)))
    pos = jnp.arange(c, dtype=_F32)
    rel = pos[:, None] - pos[None, :]
    dmask = jnp.where(rel[None] >= 0,
                      jnp.exp(jnp.maximum(rel, 0.0)[None] * lg[:, None, None]), 0.0)
    xi = jnp.exp((pos + 1.0)[None, :] * lg[:, None])[:, :, None]
    zeta = jnp.exp((c - 1.0 - pos)[None, :] * lg[:, None])[:, :, None]
    decay = jnp.broadcast_to(jnp.exp(c * lg)[:, None, None], (RET_HEADS, 1, RET_DV))
    return dmask, xi, zeta, decay


def kernel(x, w_in, w_ret_out, w_att_out, w_o, w_ffn_gate, w_ffn_up, w_ffn_down,
           g_pre_mix, g_post_mix, g_pre_ffn, g_post_ffn):
    batch, seq, d = x.shape
    depth = w_in.shape[0]
    assert d == D_MODEL and seq % TILE == 0
    cos_r, sin_r, cos_a, sin_a = _rotary_tables(seq)
    consts = _retention_consts()
    xf = x.reshape(batch * seq, d)
    for l in range(depth):
        w_in_l = w_in[l].astype(_BF16)
        proj, h = _proj_main(xf, g_pre_mix[l][None, :], w_in_l, cos_r, sin_r)
        att = _proj_att(h, w_in_l, cos_a, sin_a)
        yr = _retention(proj, consts, batch, seq)
        ya = _dilated(att, batch, seq)
        xf = _merge(xf, yr, ya, proj,
                    w_ret_out[l].astype(_BF16), w_att_out[l].astype(_BF16),
                    w_o[l].astype(_BF16), g_post_mix[l][None, :])
        xf = _ffn(xf, g_pre_ffn[l][None, :], w_ffn_gate[l].astype(_BF16),
                  w_ffn_up[l].astype(_BF16), w_ffn_down[l].astype(_BF16),
                  g_post_ffn[l][None, :])
    return xf.reshape(batch, seq, d)
```

```python
import functools
import math

import jax
import jax.numpy as jnp
from jax import lax
from jax.experimental import pallas as pl
from jax.experimental.pallas import tpu as pltpu

D_MODEL = 1024
RET_HEADS = 4
RET_DK = 256
RET_DV = 512
RET_THETA = 10000.0
ATT_GROUPS = ((128, 1), (512, 4), (2048, 16))
ATT_HPG = 4
ATT_DH = 128
ATT_ROT_HALF = 16
ATT_THETA = 500000.0
D_FF = 2816
NORM_EPS = 1e-6
NEG = -1e30

LANES = 128
MXU_W = 256
ATT_BLK = 128
TILE = 2048
SUB = 512
RET_C = 256
VMEM_LIMIT = 60 * 1024 * 1024

MAIN_W = 8192
MAIN_COLB = 1024
_RQ, _RK, _RV, _RG, _GR, _GA = 0, 1, 2, 4, 6, 7
ATT_GW = ATT_HPG * ATT_DH
ATT_COL_LO = 2 * RET_HEADS * RET_DK + 2 * RET_HEADS * RET_DV
GATE_COL_LO = ATT_COL_LO + 3 * len(ATT_GROUPS) * ATT_GW

_F32 = jnp.float32
_BF16 = jnp.bfloat16


def _dot(a, b):
    return jnp.dot(a, b, preferred_element_type=_F32)


def _dot_nt(a, b):
    return lax.dot_general(a, b, (((1,), (1,)), ((), ())), preferred_element_type=_F32)


def _dot_tn(a, b):
    return lax.dot_general(a, b, (((0,), (0,)), ((), ())), preferred_element_type=_F32)


def _rms(x, g):
    return x * lax.rsqrt(jnp.mean(x * x, axis=-1, keepdims=True) + NORM_EPS) * g


def _proj_main_kernel(x_ref, g_ref, w_ref, wlo_ref, whi_ref, cr_ref, sr_ref, o_ref, h_ref):
    j = pl.program_id(1)

    @pl.when(j == 0)
    def _():
        def body(c, carry):
            rows = pl.ds(pl.multiple_of(c * 256, 256), 256)
            h_ref[rows, :] = _rms(x_ref[rows, :], g_ref[...]).astype(_BF16)
            return carry
        lax.fori_loop(0, TILE // 256, body, 0)

    def run(epilogue):
        for c in range(TILE // SUB):
            epilogue(_dot(h_ref[c * SUB:(c + 1) * SUB, :], w_ref[...]), c)

    def plain(acc, c):
        o_ref[c * SUB:(c + 1) * SUB, :] = acc.astype(_BF16)

    def run_halves():
        half = MAIN_COLB // 2
        for c in range(TILE // SUB):
            lhs = h_ref[c * SUB:(c + 1) * SUB, :]
            o_ref[c * SUB:(c + 1) * SUB, :half] = _dot(lhs, wlo_ref[...]).astype(_BF16)
            o_ref[c * SUB:(c + 1) * SUB, half:] = _dot(lhs, whi_ref[...]).astype(_BF16)

    def ret_rotary(scale):
        def f(acc, c):
            for bi in range(SUB // ATT_BLK):
                lo = c * SUB + bi * ATT_BLK
                cos = cr_ref[lo:lo + ATT_BLK, :]
                sin = sr_ref[lo:lo + ATT_BLK, :]
                a = acc[bi * ATT_BLK:(bi + 1) * ATT_BLK, :]
                outs = []
                for hh in range(MAIN_COLB // RET_DK):
                    x1 = a[:, hh * RET_DK:hh * RET_DK + LANES]
                    x2 = a[:, hh * RET_DK + LANES:(hh + 1) * RET_DK]
                    outs.append((x1 * cos - x2 * sin) * scale)
                    outs.append((x1 * sin + x2 * cos) * scale)
                o_ref[lo:lo + ATT_BLK, :] = jnp.concatenate(outs, axis=-1).astype(_BF16)
        return f

    @pl.when(j == _RQ)
    def _():
        run(ret_rotary(1.0))

    @pl.when(j == _RK)
    def _():
        run(ret_rotary(RET_DK ** -0.5))

    @pl.when((j >= _RV) & (j < _GR))
    def _():
        run(plain)

    @pl.when(j >= _GR)
    def _():
        run_halves()


def _proj_main(x, g, w_in, layer, cos_r, sin_r):
    t = x.shape[0]
    tiles_per_seq = cos_r.shape[0] // TILE
    tab_spec = pl.BlockSpec((TILE, LANES), lambda i, j: (i % tiles_per_seq, 0))
    half = MAIN_COLB // 2
    gate_blk = GATE_COL_LO // half

    def gate_half(which):
        return pl.BlockSpec(
            (None, D_MODEL, half),
            lambda i, j: (layer, 0, gate_blk + 2 * jnp.maximum(j - _GR, 0) + which))

    return pl.pallas_call(
        _proj_main_kernel,
        grid=(t // TILE, MAIN_W // MAIN_COLB),
        in_specs=[
            pl.BlockSpec((TILE, D_MODEL), lambda i, j: (i, 0)),
            pl.BlockSpec((None, 1, D_MODEL), lambda i, j: (layer, 0, 0)),
            pl.BlockSpec((None, D_MODEL, MAIN_COLB),
                         lambda i, j: (layer, 0, jnp.minimum(j, _GR - 1))),
            gate_half(0), gate_half(1),
            tab_spec, tab_spec,
        ],
        out_specs=[
            pl.BlockSpec((TILE, MAIN_COLB), lambda i, j: (i, j)),
            pl.BlockSpec((TILE, D_MODEL), lambda i, j: (i, 0)),
        ],
        out_shape=[
            jax.ShapeDtypeStruct((t, MAIN_W), _BF16),
            jax.ShapeDtypeStruct((t, D_MODEL), _BF16),
        ],
        compiler_params=pltpu.CompilerParams(
            dimension_semantics=("arbitrary", "arbitrary"),
            vmem_limit_bytes=VMEM_LIMIT),
        name="proj_main",
    )(x, g, w_in, w_in, w_in, cos_r, sin_r)


def _proj_att_kernel(h_ref, wq_ref, wk_ref, wv_ref, ca_ref, sa_ref, o_ref, hs_ref, hp_ref):
    g = pl.program_id(1)
    n_slab = D_MODEL // LANES
    n_blk = TILE // ATT_BLK

    @pl.when(g == 0)
    def _():
        def body(c, carry):
            rows = pl.ds(pl.multiple_of(c * 256, 256), 256)
            y = h_ref[rows, :].astype(_F32)
            for k in range(n_slab):
                hs_ref[k, rows, :] = y[:, k * LANES:(k + 1) * LANES]
            return carry
        lax.fori_loop(0, TILE // 256, body, 0)

    def permute(r):
        def body(b, carry):
            if r == 16:
                start = b
            else:
                start = (b // r) * (r * ATT_BLK) + (b % r)
            dst = pl.ds(pl.multiple_of(b * ATT_BLK, ATT_BLK), ATT_BLK)
            for k in range(n_slab):
                hp_ref[dst, k * LANES:(k + 1) * LANES] = (
                    hs_ref[k, pl.ds(start, ATT_BLK, stride=r), :].astype(_BF16))
            return carry
        lax.fori_loop(0, n_blk, body, 0, unroll=2)

    def table_rows(t_ref, c, bi, r):
        if r == 1:
            return t_ref[c * SUB + bi * ATT_BLK:c * SUB + (bi + 1) * ATT_BLK, :]
        if r == 4:
            return t_ref[pl.ds(c * SUB + bi, ATT_BLK, stride=4), :]
        return t_ref[pl.ds(c * 4 + bi, ATT_BLK, stride=16), :]

    lane = lax.broadcasted_iota(jnp.int32, (ATT_BLK, LANES), 1)

    def run(lhs_ref, r):
        for c in range(TILE // SUB):
            lhs = lhs_ref[c * SUB:(c + 1) * SUB, :]
            o_ref[2, c * SUB:(c + 1) * SUB, :] = _dot(lhs, wv_ref[...]).astype(_BF16)
            for t, w_ref in enumerate((wq_ref, wk_ref)):
                acc = _dot(lhs, w_ref[...])
                for bi in range(SUB // ATT_BLK):
                    lo = c * SUB + bi * ATT_BLK
                    cos = table_rows(ca_ref, c, bi, r)
                    sin = table_rows(sa_ref, c, bi, r)
                    outs = []
                    for hh in range(ATT_HPG):
                        x = acc[bi * ATT_BLK:(bi + 1) * ATT_BLK, hh * ATT_DH:(hh + 1) * ATT_DH]
                        partner = jnp.where(lane < ATT_ROT_HALF,
                                            pltpu.roll(x, LANES - ATT_ROT_HALF, 1),
                                            pltpu.roll(x, ATT_ROT_HALF, 1))
                        outs.append(x * cos + partner * sin)
                    o_ref[t, lo:lo + ATT_BLK, :] = jnp.concatenate(outs, axis=-1).astype(_BF16)

    @pl.when(g == 0)
    def _():
        run(h_ref, 1)

    @pl.when(g == 1)
    def _():
        permute(4)
        run(hp_ref, 4)

    @pl.when(g == 2)
    def _():
        permute(16)
        run(hp_ref, 16)


def _proj_att(h, w_in, layer, cos_a, sin_a):
    t = h.shape[0]
    n_g = len(ATT_GROUPS)
    tiles_per_seq = cos_a.shape[0] // TILE
    tab_spec = pl.BlockSpec((TILE, LANES), lambda i, g: (i % tiles_per_seq, 0))
    att_blk = ATT_COL_LO // ATT_GW

    def w_spec(which):
        return pl.BlockSpec((None, D_MODEL, ATT_GW),
                            lambda i, g: (layer, 0, att_blk + which * n_g + g))

    return pl.pallas_call(
        _proj_att_kernel,
        grid=(t // TILE, n_g),
        in_specs=[
            pl.BlockSpec((TILE, D_MODEL), lambda i, g: (i, 0)),
            w_spec(0), w_spec(1), w_spec(2),
            tab_spec, tab_spec,
        ],
        out_specs=pl.BlockSpec((None, 3, TILE, ATT_GW), lambda i, g: (g, 0, i, 0)),
        out_shape=jax.ShapeDtypeStruct((n_g, 3, t, ATT_GW), _BF16),
        scratch_shapes=[
            pltpu.VMEM((D_MODEL // LANES, TILE, LANES), _F32),
            pltpu.VMEM((TILE, D_MODEL), _BF16),
        ],
        compiler_params=pltpu.CompilerParams(
            dimension_semantics=("arbitrary", "arbitrary"),
            vmem_limit_bytes=VMEM_LIMIT),
        name="proj_att",
    )(h, w_in, w_in, w_in, cos_a, sin_a)


def _retention_kernel(q_ref, k_ref, v_ref, g_ref, dm_ref, xi_ref, ze_ref, dec_ref,
                      o_ref, r_ref, *, n_chunks):
    @pl.when(pl.program_id(1) == 0)
    def _():
        r_ref[...] = jnp.zeros_like(r_ref)

    for c in range(n_chunks):
        rows = slice(c * RET_C, (c + 1) * RET_C)
        for h in range(RET_HEADS):
            q = q_ref[rows, h * RET_DK:(h + 1) * RET_DK]
            k = k_ref[rows, h * RET_DK:(h + 1) * RET_DK]
            v = v_ref[rows, h * RET_DV:(h + 1) * RET_DV]
            scores = _dot_nt(q, k) * dm_ref[h]
            qx = (q.astype(_F32) * xi_ref[h]).astype(_BF16)
            r_old = r_ref[h]
            y = _dot(jnp.concatenate([scores.astype(_BF16), qx], axis=1),
                     jnp.concatenate([v, r_old.astype(_BF16)], axis=0))
            kz = (k.astype(_F32) * ze_ref[h]).astype(_BF16)
            r_ref[h] = dec_ref[h] * r_old + _dot_tn(kz, v)
            mu = jnp.mean(y, axis=-1, keepdims=True)
            yc = y - mu
            var = jnp.mean(yc * yc, axis=-1, keepdims=True)
            yn = yc * lax.rsqrt(var + NORM_EPS)
            gate = g_ref[rows, h * RET_DV:(h + 1) * RET_DV].astype(_F32)
            o_ref[rows, h * RET_DV:(h + 1) * RET_DV] = (
                gate * jax.nn.sigmoid(gate) * yn).astype(_BF16)


def _retention(proj, consts, batch, seq, tc=1024):
    steps = seq // tc
    dm, xi, ze, dec = consts
    qk_w = RET_HEADS * RET_DK
    v_w = RET_HEADS * RET_DV
    colb = MAIN_COLB

    def const_spec(a):
        return pl.BlockSpec(a.shape, lambda b, s: (0,) * a.ndim)

    return pl.pallas_call(
        functools.partial(_retention_kernel, n_chunks=tc // RET_C),
        grid=(batch, steps),
        in_specs=[
            pl.BlockSpec((tc, qk_w), lambda b, s: (b * steps + s, _RQ * colb // qk_w)),
            pl.BlockSpec((tc, qk_w), lambda b, s: (b * steps + s, _RK * colb // qk_w)),
            pl.BlockSpec((tc, v_w), lambda b, s: (b * steps + s, _RV * colb // v_w)),
            pl.BlockSpec((tc, v_w), lambda b, s: (b * steps + s, _RG * colb // v_w)),
            const_spec(dm), const_spec(xi), const_spec(ze), const_spec(dec),
        ],
        out_specs=pl.BlockSpec((tc, v_w), lambda b, s: (b * steps + s, 0)),
        out_shape=jax.ShapeDtypeStruct((batch * seq, v_w), _BF16),
        scratch_shapes=[pltpu.VMEM((RET_HEADS, RET_DK, RET_DV), _F32)],
        compiler_params=pltpu.CompilerParams(
            dimension_semantics=("arbitrary", "arbitrary"),
            vmem_limit_bytes=VMEM_LIMIT),
        name="retention",
    )(proj, proj, proj, proj, dm, xi, ze, dec)


def _dilated_kernel(q0, k0, v0, kp0, vp0, q1, k1, v1, kp1, vp1, q2, k2, v2, kp2, vp2,
                    o_ref, out_ref, lse_ref):
    i = pl.program_id(1)
    ss = pl.program_id(2)
    n_sub = TILE // SUB
    scale = 1.0 / math.sqrt(ATT_DH)
    ci = lax.broadcasted_iota(jnp.int32, (ATT_BLK, ATT_BLK), 0)
    mi = lax.broadcasted_iota(jnp.int32, (ATT_BLK, ATT_BLK), 1)
    prev_band = mi >= ci
    cur_band = mi <= ci

    groups = (
        (q0, k0, v0, kp0, vp0, 1),
        (q1, k1, v1, kp1, vp1, 4),
        (q2, k2, v2, kp2, vp2, 16),
    )

    def item(g, bi):
        q_ref, k_ref, v_ref, kp_ref, vp_ref, r = groups[g]
        rows = slice(bi * ATT_BLK, (bi + 1) * ATT_BLK)
        if r == 1:
            if bi == 0:
                pk_ref, pv_ref, prows = kp_ref, vp_ref, slice(0, ATT_BLK)
                has_prev = (i > 0) | (ss > 0)
            else:
                pk_ref, pv_ref = k_ref, v_ref
                prows = slice((bi - 1) * ATT_BLK, bi * ATT_BLK)
                has_prev = None
            dst = pl.ds(pl.multiple_of(ss * SUB + bi * ATT_BLK, ATT_BLK), ATT_BLK)
        elif r == 4:
            pk_ref, pv_ref, prows = kp_ref, vp_ref, rows
            has_prev = (i > 0) | (ss > 0)
            dst = pl.ds(ss * SUB + bi, ATT_BLK, stride=4)
        else:
            pk_ref, pv_ref, prows = kp_ref, vp_ref, rows
            has_prev = i > 0
            dst = pl.ds(ss * 4 + bi, ATT_BLK, stride=16)
        pmask = prev_band if has_prev is None else (prev_band & has_prev)
        return dict(g=g, q_ref=q_ref, k_ref=k_ref, v_ref=v_ref, pk_ref=pk_ref, pv_ref=pv_ref,
                    rows=rows, prows=prows, pmask=pmask, dst=dst)

    def head_cols(h):
        return slice(h * ATT_DH, (h + 1) * ATT_DH)

    def score_stage(it):
        out = []
        for h in range(ATT_HPG):
            q = it["q_ref"][it["rows"], head_cols(h)]
            out.append((_dot_nt(q, it["pk_ref"][it["prows"], head_cols(h)]),
                        _dot_nt(q, it["k_ref"][it["rows"], head_cols(h)])))
        return out

    def softmax_stage(it, scores):
        out = []
        for sp, sc in scores:
            sp = jnp.where(it["pmask"], sp * scale, NEG)
            sc = jnp.where(cur_band, sc * scale, NEG)
            m = jnp.max(jnp.maximum(sp, sc), axis=-1, keepdims=True)
            pp = jnp.exp(sp - m)
            pc = jnp.exp(sc - m)
            l = jnp.sum(pp + pc, axis=-1, keepdims=True)
            out.append((pp.astype(_BF16), pc.astype(_BF16), m, l))
        return out

    def value_stage(it, probs):
        for h, (pp, pc, m, l) in enumerate(probs):
            num = (_dot(pp, it["pv_ref"][it["prows"], head_cols(h)])
                   + _dot(pc, it["v_ref"][it["rows"], head_cols(h)]))
            out_ref[it["g"], h, it["dst"], :] = num * (1.0 / l)
            lse_ref[it["g"], h, it["dst"], :] = jnp.broadcast_to(m + jnp.log(l),
                                                                 (ATT_BLK, LANES))

    items = [item(g, bi) for g in range(len(groups)) for bi in range(SUB // ATT_BLK)]
    scores = score_stage(items[0])
    for n, it in enumerate(items):
        nxt = score_stage(items[n + 1]) if n + 1 < len(items) else None
        value_stage(it, softmax_stage(it, scores))
        scores = nxt

    @pl.when(ss == n_sub - 1)
    def _():
        n_g = len(ATT_GROUPS)

        def body(c, carry):
            rows = pl.ds(pl.multiple_of(c * ATT_BLK, ATT_BLK), ATT_BLK)
            outs = []
            for h in range(ATT_HPG):
                lses = [lse_ref[g, h, rows, :] for g in range(n_g)]
                top = functools.reduce(jnp.maximum, lses)
                ws = [jnp.exp(x - top) for x in lses]
                inv = 1.0 / sum(ws)
                outs.append(sum(ws[g] * out_ref[g, h, rows, :] for g in range(n_g)) * inv)
            o_ref[rows, :] = jnp.concatenate(outs, axis=-1).astype(_BF16)
            return carry
        lax.fori_loop(0, TILE // ATT_BLK, body, 0)


def _dilated(att, batch, seq):
    tiles = seq // TILE
    n_sub = TILE // SUB
    subs_per_seq = seq // SUB
    per = SUB // ATT_BLK

    def cur(g, t):
        return pl.BlockSpec(
            (None, None, SUB, ATT_GW),
            lambda b, i, s: (g, t, b * subs_per_seq + i * n_sub + s, 0))

    def prev_sub(g, t, back):
        return pl.BlockSpec(
            (None, None, SUB, ATT_GW),
            lambda b, i, s: (g, t, b * subs_per_seq + jnp.maximum(i * n_sub + s - back, 0), 0))

    def prev_blk(g, t):
        return pl.BlockSpec(
            (None, None, ATT_BLK, ATT_GW),
            lambda b, i, s: (g, t, b * subs_per_seq * per
                             + jnp.maximum((i * n_sub + s) * per - 1, 0), 0))

    in_specs = []
    for g in range(len(ATT_GROUPS)):
        in_specs += [cur(g, 0), cur(g, 1), cur(g, 2)]
        if g == 0:
            in_specs += [prev_blk(g, 1), prev_blk(g, 2)]
        elif g == 1:
            in_specs += [prev_sub(g, 1, 1), prev_sub(g, 2, 1)]
        else:
            in_specs += [prev_sub(g, 1, n_sub), prev_sub(g, 2, n_sub)]

    return pl.pallas_call(
        _dilated_kernel,
        grid=(batch, tiles, n_sub),
        in_specs=in_specs,
        out_specs=pl.BlockSpec((TILE, ATT_GW), lambda b, i, s: (b * tiles + i, 0)),
        out_shape=jax.ShapeDtypeStruct((batch * seq, ATT_GW), _BF16),
        scratch_shapes=[
            pltpu.VMEM((len(ATT_GROUPS), ATT_HPG, TILE, LANES), _F32),
            pltpu.VMEM((len(ATT_GROUPS), ATT_HPG, TILE, LANES), _F32),
        ],
        compiler_params=pltpu.CompilerParams(
            dimension_semantics=("arbitrary", "arbitrary", "arbitrary"),
            vmem_limit_bytes=VMEM_LIMIT),
        name="dilated",
    )(*([att] * 15))


def _merge_kernel(x_ref, yr_ref, ya_ref, gr_ref, ga_ref, wr_ref, wa_ref, wo_ref, g_ref, o_ref):
    y_ret = _dot(yr_ref[...], wr_ref[...])
    y_att = _dot(ya_ref[...], wa_ref[...])
    merged = (jax.nn.sigmoid(gr_ref[...].astype(_F32)) * y_ret
              + jax.nn.sigmoid(ga_ref[...].astype(_F32)) * y_att)
    m = _dot(merged.astype(_BF16), wo_ref[...])
    o_ref[...] = x_ref[...] + _rms(m, g_ref[...])


def _resident(stacked, layer):
    shape = stacked.shape[1:]
    return pl.BlockSpec((None,) + shape, lambda i: (layer,) + (0,) * len(shape),
                        pipeline_mode=pl.Buffered(1))


def _merge(x, yr, ya, proj, wr, wa, wo, g, layer, tm=1024):
    t = x.shape[0]
    return pl.pallas_call(
        _merge_kernel,
        grid=(t // tm,),
        in_specs=[
            pl.BlockSpec((tm, D_MODEL), lambda i: (i, 0)),
            pl.BlockSpec((tm, yr.shape[1]), lambda i: (i, 0)),
            pl.BlockSpec((tm, ya.shape[1]), lambda i: (i, 0)),
            pl.BlockSpec((tm, D_MODEL), lambda i: (i, _GR * MAIN_COLB // D_MODEL)),
            pl.BlockSpec((tm, D_MODEL), lambda i: (i, _GA * MAIN_COLB // D_MODEL)),
            _resident(wr, layer), _resident(wa, layer), _resident(wo, layer),
            _resident(g, layer),
        ],
        out_specs=pl.BlockSpec((tm, D_MODEL), lambda i: (i, 0)),
        out_shape=jax.ShapeDtypeStruct((t, D_MODEL), _F32),
        compiler_params=pltpu.CompilerParams(
            dimension_semantics=("arbitrary",), vmem_limit_bytes=VMEM_LIMIT),
        name="merge",
    )(x, yr, ya, proj, proj, wr, wa, wo, g)


def _ffn_kernel(x_ref, gpre_ref, wg_ref, wu_ref, wd_ref, gpost_ref, o_ref, a_ref, *, bounds):
    x = x_ref[...]
    h = _rms(x, gpre_ref[...]).astype(_BF16)
    for lo, hi in zip(bounds[:-1], bounds[1:]):
        cols = slice(lo, hi)
        gate = _dot(h, wg_ref[:, cols])
        up = _dot(h, wu_ref[:, cols])
        a_ref[:, cols] = (gate * jax.nn.sigmoid(gate) * up).astype(_BF16)
    f = _dot(a_ref[...], wd_ref[...])
    o_ref[...] = x + _rms(f, gpost_ref[...])


def _ffn(x, gpre, wg, wu, wd, gpost, layer, tm=1024):
    t = x.shape[0]
    bounds = (0, 3 * MXU_W, 6 * MXU_W, 9 * MXU_W, D_FF)
    return pl.pallas_call(
        functools.partial(_ffn_kernel, bounds=bounds),
        grid=(t // tm,),
        in_specs=[
            pl.BlockSpec((tm, D_MODEL), lambda i: (i, 0)),
            _resident(gpre, layer), _resident(wg, layer), _resident(wu, layer),
            _resident(wd, layer), _resident(gpost, layer),
        ],
        out_specs=pl.BlockSpec((tm, D_MODEL), lambda i: (i, 0)),
        out_shape=jax.ShapeDtypeStruct((t, D_MODEL), _F32),
        scratch_shapes=[pltpu.VMEM((tm, D_FF), _BF16)],
        compiler_params=pltpu.CompilerParams(
            dimension_semantics=("arbitrary",), vmem_limit_bytes=VMEM_LIMIT),
        name="ffn",
    )(x, gpre, wg, wu, wd, gpost)


def _rotary_tables(seq):
    pos = jnp.arange(seq, dtype=_F32)[:, None]
    half = RET_DK // 2
    ang = pos * (RET_THETA ** (-jnp.arange(half, dtype=_F32) / half))[None, :]
    cos_r, sin_r = jnp.cos(ang), jnp.sin(ang)
    ang = pos * (ATT_THETA ** (-jnp.arange(ATT_ROT_HALF, dtype=_F32) / ATT_ROT_HALF))[None, :]
    cos, sin = jnp.cos(ang), jnp.sin(ang)
    rest = LANES - 2 * ATT_ROT_HALF
    cos_a = jnp.concatenate([cos, cos, jnp.ones((seq, rest), _F32)], axis=-1)
    sin_a = jnp.concatenate([-sin, sin, jnp.zeros((seq, rest), _F32)], axis=-1)
    return cos_r, sin_r, cos_a, sin_a


def _retention_consts():
    c = RET_C
    lg = jnp.log(1.0 - 2.0 ** (-5.0 - jnp.arange(RET_HEADS, dtype=_F32)))
    pos = jnp.arange(c, dtype=_F32)
    rel = pos[:, None] - pos[None, :]
    dmask = jnp.where(rel[None] >= 0,
                      jnp.exp(jnp.maximum(rel, 0.0)[None] * lg[:, None, None]), 0.0)
    xi = jnp.exp((pos + 1.0)[None, :] * lg[:, None])[:, :, None]
    zeta = jnp.exp((c - 1.0 - pos)[None, :] * lg[:, None])[:, :, None]
    decay = jnp.broadcast_to(jnp.exp(c * lg)[:, None, None], (RET_HEADS, 1, RET_DV))
    return dmask, xi, zeta, decay


def kernel(x, w_in, w_ret_out, w_att_out, w_o, w_ffn_gate, w_ffn_up, w_ffn_down,
           g_pre_mix, g_post_mix, g_pre_ffn, g_post_ffn):
    batch, seq, d = x.shape
    depth = w_in.shape[0]
    assert d == D_MODEL and seq % TILE == 0
    cos_r, sin_r, cos_a, sin_a = _rotary_tables(seq)
    consts = _retention_consts()
    xf = x.reshape(batch * seq, d)
    w_in, w_ret_out, w_att_out, w_o, w_ffn_gate, w_ffn_up, w_ffn_down = (
        w.astype(_BF16) for w in (w_in, w_ret_out, w_att_out, w_o,
                                  w_ffn_gate, w_ffn_up, w_ffn_down))
    g_pre_mix, g_post_mix, g_pre_ffn, g_post_ffn = (
        g[:, None, :] for g in (g_pre_mix, g_post_mix, g_pre_ffn, g_post_ffn))
    for l in range(depth):
        proj, h = _proj_main(xf, g_pre_mix, w_in, l, cos_r, sin_r)
        att = _proj_att(h, w_in, l, cos_a, sin_a)
        yr = _retention(proj, consts, batch, seq)
        ya = _dilated(att, batch, seq)
        xf = _merge(xf, yr, ya, proj, w_ret_out, w_att_out, w_o, g_post_mix, l)
        xf = _ffn(xf, g_pre_ffn, w_ffn_gate, w_ffn_up, w_ffn_down, g_post_ffn, l)
    return xf.reshape(batch, seq, d)
```

```python
import functools
import math

import jax
import jax.numpy as jnp
from jax import lax
from jax.experimental import pallas as pl
from jax.experimental.pallas import tpu as pltpu

D_MODEL = 1024
RET_HEADS = 4
RET_DK = 256
RET_DV = 512
RET_THETA = 10000.0
ATT_GROUPS = ((128, 1), (512, 4), (2048, 16))
ATT_HPG = 4
ATT_DH = 128
ATT_ROT_HALF = 16
ATT_THETA = 500000.0
D_FF = 2816
NORM_EPS = 1e-6
NEG = -1e30

LANES = 128
MXU_W = 256
ATT_BLK = 128
TILE = 2048
SUB = 512
RET_C = 256
VMEM_LIMIT = 60 * 1024 * 1024

MAIN_W = 8192
MAIN_COLB = 1024
_RQ, _RK, _RV, _RG, _GR, _GA = 0, 1, 2, 4, 6, 7
ATT_GW = ATT_HPG * ATT_DH
ATT_K0, ATT_V0, ATT_Q0 = 0, ATT_GW, 2 * ATT_GW
ATT_COL_LO = 2 * RET_HEADS * RET_DK + 2 * RET_HEADS * RET_DV
GATE_COL_LO = ATT_COL_LO + 3 * len(ATT_GROUPS) * ATT_GW

_F32 = jnp.float32
_BF16 = jnp.bfloat16


def _dot(a, b):
    return jnp.dot(a, b, preferred_element_type=_F32)


def _dot_nt(a, b):
    return lax.dot_general(a, b, (((1,), (1,)), ((), ())), preferred_element_type=_F32)


def _dot_tn(a, b):
    return lax.dot_general(a, b, (((0,), (0,)), ((), ())), preferred_element_type=_F32)


def _rms(x, g):
    return x * lax.rsqrt(jnp.mean(x * x, axis=-1, keepdims=True) + NORM_EPS) * g


def _proj_main_kernel(x_ref, g_ref, w_ref, wlo_ref, whi_ref, cr_ref, sr_ref, o_ref, h_ref):
    j = pl.program_id(1)

    @pl.when(j == 0)
    def _():
        def body(c, carry):
            rows = pl.ds(pl.multiple_of(c * 256, 256), 256)
            h_ref[rows, :] = _rms(x_ref[rows, :], g_ref[...]).astype(_BF16)
            return carry
        lax.fori_loop(0, TILE // 256, body, 0)

    def run(epilogue):
        for c in range(TILE // SUB):
            epilogue(_dot(h_ref[c * SUB:(c + 1) * SUB, :], w_ref[...]), c)

    def plain(acc, c):
        o_ref[c * SUB:(c + 1) * SUB, :] = acc.astype(_BF16)

    def run_halves():
        half = MAIN_COLB // 2
        for c in range(TILE // SUB):
            lhs = h_ref[c * SUB:(c + 1) * SUB, :]
            o_ref[c * SUB:(c + 1) * SUB, :half] = _dot(lhs, wlo_ref[...]).astype(_BF16)
            o_ref[c * SUB:(c + 1) * SUB, half:] = _dot(lhs, whi_ref[...]).astype(_BF16)

    def ret_rotary(scale):
        def f(acc, c):
            for bi in range(SUB // ATT_BLK):
                lo = c * SUB + bi * ATT_BLK
                cos = cr_ref[lo:lo + ATT_BLK, :]
                sin = sr_ref[lo:lo + ATT_BLK, :]
                a = acc[bi * ATT_BLK:(bi + 1) * ATT_BLK, :]
                outs = []
                for hh in range(MAIN_COLB // RET_DK):
                    x1 = a[:, hh * RET_DK:hh * RET_DK + LANES]
                    x2 = a[:, hh * RET_DK + LANES:(hh + 1) * RET_DK]
                    outs.append((x1 * cos - x2 * sin) * scale)
                    outs.append((x1 * sin + x2 * cos) * scale)
                o_ref[lo:lo + ATT_BLK, :] = jnp.concatenate(outs, axis=-1).astype(_BF16)
        return f

    @pl.when(j == _RQ)
    def _():
        run(ret_rotary(1.0))

    @pl.when(j == _RK)
    def _():
        run(ret_rotary(RET_DK ** -0.5))

    @pl.when((j >= _RV) & (j < _GR))
    def _():
        run(plain)

    @pl.when(j >= _GR)
    def _():
        run_halves()


def _proj_main(x, g, w_in, layer, cos_r, sin_r):
    t = x.shape[0]
    tiles_per_seq = cos_r.shape[0] // TILE
    tab_spec = pl.BlockSpec((TILE, LANES), lambda i, j: (i % tiles_per_seq, 0))
    half = MAIN_COLB // 2
    gate_blk = GATE_COL_LO // half

    def gate_half(which):
        return pl.BlockSpec(
            (None, D_MODEL, half),
            lambda i, j: (layer, 0, gate_blk + 2 * jnp.maximum(j - _GR, 0) + which))

    return pl.pallas_call(
        _proj_main_kernel,
        grid=(t // TILE, MAIN_W // MAIN_COLB),
        in_specs=[
            pl.BlockSpec((TILE, D_MODEL), lambda i, j: (i, 0)),
            pl.BlockSpec((None, 1, D_MODEL), lambda i, j: (layer, 0, 0)),
            pl.BlockSpec((None, D_MODEL, MAIN_COLB),
                         lambda i, j: (layer, 0, jnp.minimum(j, _GR - 1))),
            gate_half(0), gate_half(1),
            tab_spec, tab_spec,
        ],
        out_specs=[
            pl.BlockSpec((TILE, MAIN_COLB), lambda i, j: (i, j)),
            pl.BlockSpec((TILE, D_MODEL), lambda i, j: (i, 0)),
        ],
        out_shape=[
            jax.ShapeDtypeStruct((t, MAIN_W), _BF16),
            jax.ShapeDtypeStruct((t, D_MODEL), _BF16),
        ],
        compiler_params=pltpu.CompilerParams(
            dimension_semantics=("arbitrary", "arbitrary"),
            vmem_limit_bytes=VMEM_LIMIT),
        name="proj_main",
    )(x, g, w_in, w_in, w_in, cos_r, sin_r)


def _proj_att_kernel(h_ref, wq_ref, wk_ref, wv_ref, ca_ref, sa_ref, o_ref, hs_ref, hp_ref):
    g = pl.program_id(1)
    n_slab = D_MODEL // LANES
    n_blk = TILE // ATT_BLK

    @pl.when(g == 0)
    def _():
        def body(c, carry):
            rows = pl.ds(pl.multiple_of(c * 256, 256), 256)
            y = h_ref[rows, :].astype(_F32)
            for k in range(n_slab):
                hs_ref[k, rows, :] = y[:, k * LANES:(k + 1) * LANES]
            return carry
        lax.fori_loop(0, TILE // 256, body, 0)

    def permute(r):
        def body(b, carry):
            if r == 16:
                start = b
            else:
                start = (b // r) * (r * ATT_BLK) + (b % r)
            dst = pl.ds(pl.multiple_of(b * ATT_BLK, ATT_BLK), ATT_BLK)
            for k in range(n_slab):
                hp_ref[dst, k * LANES:(k + 1) * LANES] = (
                    hs_ref[k, pl.ds(start, ATT_BLK, stride=r), :].astype(_BF16))
            return carry
        lax.fori_loop(0, n_blk, body, 0, unroll=2)

    def table_rows(t_ref, c, bi, r):
        if r == 1:
            return t_ref[c * SUB + bi * ATT_BLK:c * SUB + (bi + 1) * ATT_BLK, :]
        if r == 4:
            return t_ref[pl.ds(c * SUB + bi, ATT_BLK, stride=4), :]
        return t_ref[pl.ds(c * 4 + bi, ATT_BLK, stride=16), :]

    lane = lax.broadcasted_iota(jnp.int32, (ATT_BLK, LANES), 1)

    def run(lhs_ref, r):
        for c in range(TILE // SUB):
            lhs = lhs_ref[c * SUB:(c + 1) * SUB, :]
            o_ref[c * SUB:(c + 1) * SUB, ATT_V0:ATT_V0 + ATT_GW] = (
                _dot(lhs, wv_ref[...]).astype(_BF16))
            for col0, w_ref in ((ATT_Q0, wq_ref), (ATT_K0, wk_ref)):
                acc = _dot(lhs, w_ref[...])
                for bi in range(SUB // ATT_BLK):
                    lo = c * SUB + bi * ATT_BLK
                    cos = table_rows(ca_ref, c, bi, r)
                    sin = table_rows(sa_ref, c, bi, r)
                    outs = []
                    for hh in range(ATT_HPG):
                        x = acc[bi * ATT_BLK:(bi + 1) * ATT_BLK, hh * ATT_DH:(hh + 1) * ATT_DH]
                        partner = jnp.where(lane < ATT_ROT_HALF,
                                            pltpu.roll(x, LANES - ATT_ROT_HALF, 1),
                                            pltpu.roll(x, ATT_ROT_HALF, 1))
                        outs.append(x * cos + partner * sin)
                    o_ref[lo:lo + ATT_BLK, col0:col0 + ATT_GW] = (
                        jnp.concatenate(outs, axis=-1).astype(_BF16))

    @pl.when(g == 0)
    def _():
        run(h_ref, 1)

    @pl.when(g == 1)
    def _():
        permute(4)
        run(hp_ref, 4)

    @pl.when(g == 2)
    def _():
        permute(16)
        run(hp_ref, 16)


def _proj_att(h, w_in, layer, cos_a, sin_a):
    t = h.shape[0]
    n_g = len(ATT_GROUPS)
    tiles_per_seq = cos_a.shape[0] // TILE
    tab_spec = pl.BlockSpec((TILE, LANES), lambda i, g: (i % tiles_per_seq, 0))
    att_blk = ATT_COL_LO // ATT_GW

    def w_spec(which):
        return pl.BlockSpec((None, D_MODEL, ATT_GW),
                            lambda i, g: (layer, 0, att_blk + which * n_g + g))

    return pl.pallas_call(
        _proj_att_kernel,
        grid=(t // TILE, n_g),
        in_specs=[
            pl.BlockSpec((TILE, D_MODEL), lambda i, g: (i, 0)),
            w_spec(0), w_spec(1), w_spec(2),
            tab_spec, tab_spec,
        ],
        out_specs=pl.BlockSpec((None, TILE, 3 * ATT_GW), lambda i, g: (g, i, 0)),
        out_shape=jax.ShapeDtypeStruct((n_g, t, 3 * ATT_GW), _BF16),
        scratch_shapes=[
            pltpu.VMEM((D_MODEL // LANES, TILE, LANES), _F32),
            pltpu.VMEM((TILE, D_MODEL), _BF16),
        ],
        compiler_params=pltpu.CompilerParams(
            dimension_semantics=("arbitrary", "arbitrary"),
            vmem_limit_bytes=VMEM_LIMIT),
        name="proj_att",
    )(h, w_in, w_in, w_in, cos_a, sin_a)


def _dilated_kernel(cur0, prev0, cur1, prev1, cur2, prev2, o_ref, out_ref, lse_ref):
    i = pl.program_id(1)
    ss = pl.program_id(2)
    n_sub = TILE // SUB
    scale = 1.0 / math.sqrt(ATT_DH)
    ci = lax.broadcasted_iota(jnp.int32, (ATT_BLK, ATT_BLK), 0)
    mi = lax.broadcasted_iota(jnp.int32, (ATT_BLK, ATT_BLK), 1)
    prev_band = mi >= ci
    cur_band = mi <= ci

    groups = ((cur0, prev0, 1), (cur1, prev1, 4), (cur2, prev2, 16))

    def item(g, bi):
        cur_ref, prev_ref, r = groups[g]
        rows = slice(bi * ATT_BLK, (bi + 1) * ATT_BLK)
        if r == 1:
            if bi == 0:
                p_ref, prows = prev_ref, slice(0, ATT_BLK)
                has_prev = (i > 0) | (ss > 0)
            else:
                p_ref, prows = cur_ref, slice((bi - 1) * ATT_BLK, bi * ATT_BLK)
                has_prev = None
            dst = pl.ds(pl.multiple_of(ss * SUB + bi * ATT_BLK, ATT_BLK), ATT_BLK)
        elif r == 4:
            p_ref, prows = prev_ref, rows
            has_prev = (i > 0) | (ss > 0)
            dst = pl.ds(ss * SUB + bi, ATT_BLK, stride=4)
        else:
            p_ref, prows = prev_ref, rows
            has_prev = i > 0
            dst = pl.ds(ss * 4 + bi, ATT_BLK, stride=16)
        pmask = prev_band if has_prev is None else (prev_band & has_prev)
        return dict(g=g, cur_ref=cur_ref, p_ref=p_ref, rows=rows, prows=prows,
                    pmask=pmask, dst=dst)

    def head_cols(col0, h):
        return slice(col0 + h * ATT_DH, col0 + (h + 1) * ATT_DH)

    def score_stage(it):
        out = []
        for h in range(ATT_HPG):
            q = it["cur_ref"][it["rows"], head_cols(ATT_Q0, h)]
            out.append((_dot_nt(q, it["p_ref"][it["prows"], head_cols(ATT_K0, h)]),
                        _dot_nt(q, it["cur_ref"][it["rows"], head_cols(ATT_K0, h)])))
        return out

    def softmax_stage(it, scores):
        out = []
        for sp, sc in scores:
            sp = jnp.where(it["pmask"], sp * scale, NEG)
            sc = jnp.where(cur_band, sc * scale, NEG)
            m = jnp.max(jnp.maximum(sp, sc), axis=-1, keepdims=True)
            pp = jnp.exp(sp - m)
            pc = jnp.exp(sc - m)
            l = jnp.sum(pp + pc, axis=-1, keepdims=True)
            out.append((pp.astype(_BF16), pc.astype(_BF16), m, l))
        return out

    def value_stage(it, probs):
        for h, (pp, pc, m, l) in enumerate(probs):
            num = (_dot(pp, it["p_ref"][it["prows"], head_cols(ATT_V0, h)])
                   + _dot(pc, it["cur_ref"][it["rows"], head_cols(ATT_V0, h)]))
            out_ref[it["g"], h, it["dst"], :] = num * (1.0 / l)
            lse_ref[it["g"], h, it["dst"], :] = jnp.broadcast_to(m + jnp.log(l),
                                                                 (ATT_BLK, LANES))

    items = [item(g, bi) for g in range(len(groups)) for bi in range(SUB // ATT_BLK)]
    scores = score_stage(items[0])
    for n, it in enumerate(items):
        nxt = score_stage(items[n + 1]) if n + 1 < len(items) else None
        value_stage(it, softmax_stage(it, scores))
        scores = nxt

    @pl.when(ss == n_sub - 1)
    def _():
        n_g = len(ATT_GROUPS)

        def body(c, carry):
            rows = pl.ds(pl.multiple_of(c * ATT_BLK, ATT_BLK), ATT_BLK)
            outs = []
            for h in range(ATT_HPG):
                lses = [lse_ref[g, h, rows, :] for g in range(n_g)]
                top = functools.reduce(jnp.maximum, lses)
                ws = [jnp.exp(x - top) for x in lses]
                inv = 1.0 / sum(ws)
                outs.append(sum(ws[g] * out_ref[g, h, rows, :] for g in range(n_g)) * inv)
            o_ref[rows, :] = jnp.concatenate(outs, axis=-1).astype(_BF16)
            return carry
        lax.fori_loop(0, TILE // ATT_BLK, body, 0)


def _dilated(att, batch, seq):
    tiles = seq // TILE
    n_sub = TILE // SUB
    subs_per_seq = seq // SUB
    per = SUB // ATT_BLK

    def cur(g):
        return pl.BlockSpec(
            (None, SUB, 3 * ATT_GW),
            lambda b, i, s: (g, b * subs_per_seq + i * n_sub + s, 0))

    def prev_sub(g, back):
        return pl.BlockSpec(
            (None, SUB, 2 * ATT_GW),
            lambda b, i, s: (g, b * subs_per_seq + jnp.maximum(i * n_sub + s - back, 0), 0))

    def prev_blk(g):
        return pl.BlockSpec(
            (None, ATT_BLK, 2 * ATT_GW),
            lambda b, i, s: (g, b * subs_per_seq * per
                             + jnp.maximum((i * n_sub + s) * per - 1, 0), 0))

    in_specs = [cur(0), prev_blk(0), cur(1), prev_sub(1, 1), cur(2), prev_sub(2, n_sub)]

    return pl.pallas_call(
        _dilated_kernel,
        grid=(batch, tiles, n_sub),
        in_specs=in_specs,
        out_specs=pl.BlockSpec((TILE, ATT_GW), lambda b, i, s: (b * tiles + i, 0)),
        out_shape=jax.ShapeDtypeStruct((batch * seq, ATT_GW), _BF16),
        scratch_shapes=[
            pltpu.VMEM((len(ATT_GROUPS), ATT_HPG, TILE, LANES), _F32),
            pltpu.VMEM((len(ATT_GROUPS), ATT_HPG, TILE, LANES), _F32),
        ],
        compiler_params=pltpu.CompilerParams(
            dimension_semantics=("arbitrary", "arbitrary", "arbitrary"),
            vmem_limit_bytes=VMEM_LIMIT),
        name="dilated",
    )(*([att] * 6))


def _ret_merge_kernel(q_ref, k_ref, v_ref, g_ref, dm_ref, xi_ref, ze_ref, dec_ref,
                      x_ref, ya_ref, gr_ref, ga_ref, wr_ref, wa_ref, wo_ref, gn_ref,
                      o_ref, r_ref, yr_prev, yr_cur, *, n_chunks, blocks_per_seq):
    s = pl.program_id(0)

    @pl.when(s % blocks_per_seq == 0)
    def _():
        r_ref[...] = jnp.zeros_like(r_ref)

    @pl.when(s == 0)
    def _():
        yr_prev[...] = jnp.zeros_like(yr_prev)

    y_ret = _dot(yr_prev[...], wr_ref[...])
    y_att = _dot(ya_ref[...], wa_ref[...])
    merged = (jax.nn.sigmoid(gr_ref[...].astype(_F32)) * y_ret
              + jax.nn.sigmoid(ga_ref[...].astype(_F32)) * y_att)
    m = _dot(merged.astype(_BF16), wo_ref[...])
    o_ref[...] = x_ref[...] + _rms(m, gn_ref[...])

    for c in range(n_chunks):
        rows = slice(c * RET_C, (c + 1) * RET_C)
        for h in range(RET_HEADS):
            q = q_ref[rows, h * RET_DK:(h + 1) * RET_DK]
            k = k_ref[rows, h * RET_DK:(h + 1) * RET_DK]
            v = v_ref[rows, h * RET_DV:(h + 1) * RET_DV]
            scores = _dot_nt(q, k) * dm_ref[h]
            qx = (q.astype(_F32) * xi_ref[h]).astype(_BF16)
            r_old = r_ref[h]
            y = _dot(jnp.concatenate([scores.astype(_BF16), qx], axis=1),
                     jnp.concatenate([v, r_old.astype(_BF16)], axis=0))
            kz = (k.astype(_F32) * ze_ref[h]).astype(_BF16)
            r_ref[h] = dec_ref[h] * r_old + _dot_tn(kz, v)
            mu = jnp.mean(y, axis=-1, keepdims=True)
            yc = y - mu
            var = jnp.mean(yc * yc, axis=-1, keepdims=True)
            yn = yc * lax.rsqrt(var + NORM_EPS)
            gate = g_ref[rows, h * RET_DV:(h + 1) * RET_DV].astype(_F32)
            yr_cur[rows, h * RET_DV:(h + 1) * RET_DV] = (
                gate * jax.nn.sigmoid(gate) * yn).astype(_BF16)

    yr_prev[...] = yr_cur[...]


def _resident(stacked, layer):
    shape = stacked.shape[1:]
    return pl.BlockSpec((None,) + shape, lambda i: (layer,) + (0,) * len(shape),
                        pipeline_mode=pl.Buffered(1))


def _ret_merge(x, proj, ya, consts, wr, wa, wo, g, layer, seq, tc=512):
    t = x.shape[0]
    n_blocks = t // tc
    dm, xi, ze, dec = consts
    qk_w = RET_HEADS * RET_DK
    v_w = RET_HEADS * RET_DV
    colb = MAIN_COLB

    def ret_blk(width, col):
        return pl.BlockSpec((tc, width), lambda s: (jnp.minimum(s, n_blocks - 1), col))

    def merge_blk(width, col):
        return pl.BlockSpec((tc, width), lambda s: (jnp.maximum(s - 1, 0), col))

    def const_spec(a):
        return pl.BlockSpec(a.shape, lambda s: (0,) * a.ndim)

    return pl.pallas_call(
        functools.partial(_ret_merge_kernel, n_chunks=tc // RET_C,
                          blocks_per_seq=seq // tc),
        grid=(n_blocks + 1,),
        in_specs=[
            ret_blk(qk_w, _RQ * colb // qk_w), ret_blk(qk_w, _RK * colb // qk_w),
            ret_blk(v_w, _RV * colb // v_w), ret_blk(v_w, _RG * colb // v_w),
            const_spec(dm), const_spec(xi), const_spec(ze), const_spec(dec),
            merge_blk(D_MODEL, 0), merge_blk(ya.shape[1], 0),
            merge_blk(D_MODEL, _GR * colb // D_MODEL), merge_blk(D_MODEL, _GA * colb // D_MODEL),
            _resident(wr, layer), _resident(wa, layer), _resident(wo, layer),
            _resident(g, layer),
        ],
        out_specs=merge_blk(D_MODEL, 0),
        out_shape=jax.ShapeDtypeStruct((t, D_MODEL), _F32),
        scratch_shapes=[
            pltpu.VMEM((RET_HEADS, RET_DK, RET_DV), _F32),
            pltpu.VMEM((tc, v_w), _BF16),
            pltpu.VMEM((tc, v_w), _BF16),
        ],
        compiler_params=pltpu.CompilerParams(
            dimension_semantics=("arbitrary",), vmem_limit_bytes=VMEM_LIMIT),
        name="ret_merge",
    )(proj, proj, proj, proj, dm, xi, ze, dec, x, ya, proj, proj, wr, wa, wo, g)


def _ffn_kernel(x_ref, gpre_ref, wg_ref, wu_ref, wd_ref, gpost_ref, o_ref, a_ref, *, bounds):
    x = x_ref[...]
    h = _rms(x, gpre_ref[...]).astype(_BF16)
    for lo, hi in zip(bounds[:-1], bounds[1:]):
        cols = slice(lo, hi)
        gate = _dot(h, wg_ref[:, cols])
        up = _dot(h, wu_ref[:, cols])
        a_ref[:, cols] = (gate * jax.nn.sigmoid(gate) * up).astype(_BF16)
    f = _dot(a_ref[...], wd_ref[...])
    o_ref[...] = x + _rms(f, gpost_ref[...])


def _ffn(x, gpre, wg, wu, wd, gpost, layer, tm=1024):
    t = x.shape[0]
    bounds = (0, 3 * MXU_W, 6 * MXU_W, 9 * MXU_W, D_FF)
    return pl.pallas_call(
        functools.partial(_ffn_kernel, bounds=bounds),
        grid=(t // tm,),
        in_specs=[
            pl.BlockSpec((tm, D_MODEL), lambda i: (i, 0)),
            _resident(gpre, layer), _resident(wg, layer), _resident(wu, layer),
            _resident(wd, layer), _resident(gpost, layer),
        ],
        out_specs=pl.BlockSpec((tm, D_MODEL), lambda i: (i, 0)),
        out_shape=jax.ShapeDtypeStruct((t, D_MODEL), _F32),
        scratch_shapes=[pltpu.VMEM((tm, D_FF), _BF16)],
        compiler_params=pltpu.CompilerParams(
            dimension_semantics=("arbitrary",), vmem_limit_bytes=VMEM_LIMIT),
        name="ffn",
    )(x, gpre, wg, wu, wd, gpost)


def _cos_sin(seq, inv_freq, coarse=64):
    hi = (jnp.arange(seq // coarse, dtype=_F32) * coarse)[:, None] * inv_freq[None, :]
    lo = jnp.arange(coarse, dtype=_F32)[:, None] * inv_freq[None, :]
    ch, sh = jnp.cos(hi)[:, None, :], jnp.sin(hi)[:, None, :]
    cl, sl = jnp.cos(lo)[None, :, :], jnp.sin(lo)[None, :, :]
    n = inv_freq.shape[0]
    return (ch * cl - sh * sl).reshape(seq, n), (sh * cl + ch * sl).reshape(seq, n)


def _rotary_tables(seq):
    half = RET_DK // 2
    cos_r, sin_r = _cos_sin(seq, RET_THETA ** (-jnp.arange(half, dtype=_F32) / half))
    cos, sin = _cos_sin(
        seq, ATT_THETA ** (-jnp.arange(ATT_ROT_HALF, dtype=_F32) / ATT_ROT_HALF))
    rest = LANES - 2 * ATT_ROT_HALF
    cos_a = jnp.concatenate([cos, cos, jnp.ones((seq, rest), _F32)], axis=-1)
    sin_a = jnp.concatenate([-sin, sin, jnp.zeros((seq, rest), _F32)], axis=-1)
    return cos_r, sin_r, cos_a, sin_a


def _retention_consts():
    c = RET_C
    lg = jnp.log(1.0 - 2.0 ** (-5.0 - jnp.arange(RET_HEADS, dtype=_F32)))
    pos = jnp.arange(c, dtype=_F32)
    rel = pos[:, None] - pos[None, :]
    dmask = jnp.where(rel[None] >= 0,
                      jnp.exp(jnp.maximum(rel, 0.0)[None] * lg[:, None, None]), 0.0)
    xi = jnp.exp((pos + 1.0)[None, :] * lg[:, None])[:, :, None]
    zeta = jnp.exp((c - 1.0 - pos)[None, :] * lg[:, None])[:, :, None]
    decay = jnp.broadcast_to(jnp.exp(c * lg)[:, None, None], (RET_HEADS, 1, RET_DV))
    return dmask, xi, zeta, decay


def kernel(x, w_in, w_ret_out, w_att_out, w_o, w_ffn_gate, w_ffn_up, w_ffn_down,
           g_pre_mix, g_post_mix, g_pre_ffn, g_post_ffn):
    batch, seq, d = x.shape
    depth = w_in.shape[0]
    assert d == D_MODEL and seq % TILE == 0
    cos_r, sin_r, cos_a, sin_a = _rotary_tables(seq)
    consts = _retention_consts()
    xf = x.reshape(batch * seq, d)
    w_in, w_ret_out, w_att_out, w_o, w_ffn_gate, w_ffn_up, w_ffn_down = (
        w.astype(_BF16) for w in (w_in, w_ret_out, w_att_out, w_o,
                                  w_ffn_gate, w_ffn_up, w_ffn_down))
    g_pre_mix, g_post_mix, g_pre_ffn, g_post_ffn = (
        g[:, None, :] for g in (g_pre_mix, g_post_mix, g_pre_ffn, g_post_ffn))
    for l in range(depth):
        proj, h = _proj_main(xf, g_pre_mix, w_in, l, cos_r, sin_r)
        att = _proj_att(h, w_in, l, cos_a, sin_a)
        ya = _dilated(att, batch, seq)
        xf = _ret_merge(xf, proj, ya, consts, w_ret_out, w_att_out, w_o, g_post_mix, l, seq)
        xf = _ffn(xf, g_pre_ffn, w_ffn_gate, w_ffn_up, w_ffn_down, g_post_ffn, l)
    return xf.reshape(batch, seq, d)
```

```python
import functools
import math

import jax
import jax.numpy as jnp
from jax import lax
from jax.experimental import pallas as pl
from jax.experimental.pallas import tpu as pltpu

D_MODEL = 1024
RET_HEADS = 4
RET_DK = 256
RET_DV = 512
RET_THETA = 10000.0
ATT_GROUPS = ((128, 1), (512, 4), (2048, 16))
ATT_HPG = 4
ATT_DH = 128
ATT_ROT_HALF = 16
ATT_THETA = 500000.0
D_FF = 2816
NORM_EPS = 1e-6
NEG = -1e30

LANES = 128
MXU_W = 256
ATT_BLK = 128
TILE = 2048
SUB = 512
RET_C = 256
MERGE_PIECES = 4
VMEM_LIMIT = 60 * 1024 * 1024

MAIN_W = 8192
MAIN_COLB = 1024
_RQ, _RK, _RV, _RG, _GR, _GA = 0, 1, 2, 4, 6, 7
ATT_GW = ATT_HPG * ATT_DH
ATT_K0, ATT_V0, ATT_Q0 = 0, ATT_GW, 2 * ATT_GW
ATT_COL_LO = 2 * RET_HEADS * RET_DK + 2 * RET_HEADS * RET_DV
GATE_COL_LO = ATT_COL_LO + 3 * len(ATT_GROUPS) * ATT_GW

_F32 = jnp.float32
_BF16 = jnp.bfloat16


def _dot(a, b):
    return jnp.dot(a, b, preferred_element_type=_F32)


def _dot_nt(a, b):
    return lax.dot_general(a, b, (((1,), (1,)), ((), ())), preferred_element_type=_F32)


def _dot_tn(a, b):
    return lax.dot_general(a, b, (((0,), (0,)), ((), ())), preferred_element_type=_F32)


def _res16(sub, blk):
    return 4 * blk + sub


def _rms(x, g):
    return x * lax.rsqrt(jnp.mean(x * x, axis=-1, keepdims=True) + NORM_EPS) * g


def _proj_main_kernel(x_ref, g_ref, w_ref, wlo_ref, whi_ref, cr_ref, sr_ref, o_ref, h_ref):
    j = pl.program_id(1)

    def norm_chunk(c):
        for r in range(c * SUB, (c + 1) * SUB, 256):
            h_ref[r:r + 256, :] = _rms(x_ref[r:r + 256, :], g_ref[...]).astype(_BF16)

    def run(epilogue, prologue=None):
        for c in range(TILE // SUB):
            if prologue is not None:
                prologue(c)
            epilogue(_dot(h_ref[c * SUB:(c + 1) * SUB, :], w_ref[...]), c)

    def plain(acc, c):
        o_ref[c * SUB:(c + 1) * SUB, :] = acc.astype(_BF16)

    def run_halves():
        half = MAIN_COLB // 2
        for c in range(TILE // SUB):
            lhs = h_ref[c * SUB:(c + 1) * SUB, :]
            o_ref[c * SUB:(c + 1) * SUB, :half] = _dot(lhs, wlo_ref[...]).astype(_BF16)
            o_ref[c * SUB:(c + 1) * SUB, half:] = _dot(lhs, whi_ref[...]).astype(_BF16)

    def ret_rotary(scale):
        def f(acc, c):
            for bi in range(SUB // ATT_BLK):
                lo = c * SUB + bi * ATT_BLK
                cos = cr_ref[lo:lo + ATT_BLK, :]
                sin = sr_ref[lo:lo + ATT_BLK, :]
                a = acc[bi * ATT_BLK:(bi + 1) * ATT_BLK, :]
                outs = []
                for hh in range(MAIN_COLB // RET_DK):
                    x1 = a[:, hh * RET_DK:hh * RET_DK + LANES]
                    x2 = a[:, hh * RET_DK + LANES:(hh + 1) * RET_DK]
                    outs.append((x1 * cos - x2 * sin) * scale)
                    outs.append((x1 * sin + x2 * cos) * scale)
                o_ref[lo:lo + ATT_BLK, :] = jnp.concatenate(outs, axis=-1).astype(_BF16)
        return f

    @pl.when(j == _RQ)
    def _():
        run(ret_rotary(1.0), prologue=norm_chunk)

    @pl.when(j == _RK)
    def _():
        run(ret_rotary(RET_DK ** -0.5))

    @pl.when((j >= _RV) & (j < _GR))
    def _():
        run(plain)

    @pl.when(j >= _GR)
    def _():
        run_halves()


def _proj_main(x, g, w_in, layer, cos_r, sin_r):
    t = x.shape[0]
    tiles_per_seq = cos_r.shape[0] // TILE
    tab_spec = pl.BlockSpec((TILE, LANES), lambda i, j: (i % tiles_per_seq, 0))
    half = MAIN_COLB // 2
    gate_blk = GATE_COL_LO // half

    def gate_half(which):
        return pl.BlockSpec(
            (None, D_MODEL, half),
            lambda i, j: (layer, 0, gate_blk + 2 * jnp.maximum(j - _GR, 0) + which))

    return pl.pallas_call(
        _proj_main_kernel,
        grid=(t // TILE, MAIN_W // MAIN_COLB),
        in_specs=[
            pl.BlockSpec((TILE, D_MODEL), lambda i, j: (i, 0)),
            pl.BlockSpec((None, 1, D_MODEL), lambda i, j: (layer, 0, 0)),
            pl.BlockSpec((None, D_MODEL, MAIN_COLB),
                         lambda i, j: (layer, 0, jnp.minimum(j, _GR - 1))),
            gate_half(0), gate_half(1),
            tab_spec, tab_spec,
        ],
        out_specs=[
            pl.BlockSpec((TILE, MAIN_COLB), lambda i, j: (i, j)),
            pl.BlockSpec((TILE, D_MODEL), lambda i, j: (i, 0)),
        ],
        out_shape=[
            jax.ShapeDtypeStruct((t, MAIN_W), _BF16),
            jax.ShapeDtypeStruct((t, D_MODEL), _BF16),
        ],
        compiler_params=pltpu.CompilerParams(
            dimension_semantics=("arbitrary", "arbitrary"),
            vmem_limit_bytes=VMEM_LIMIT),
        name="proj_main",
    )(x, g, w_in, w_in, w_in, cos_r, sin_r)


def _proj_att_kernel(h_ref, wq_ref, wk_ref, wv_ref, ca_ref, sa_ref, o_ref,
                     hs_ref, hq_ref, hp_ref):
    g = pl.program_id(1)
    n_slab = D_MODEL // LANES
    n_blk = TILE // ATT_BLK

    def build_slabs():
        for c in range(TILE // 256):
            y = h_ref[c * 256:(c + 1) * 256, :].astype(_F32)
            for k in range(n_slab):
                hs_ref[k, c * 256:(c + 1) * 256, :] = y[:, k * LANES:(k + 1) * LANES]

    def permute4():
        for b in range(n_blk):
            start = (b // 4) * (4 * ATT_BLK) + (b % 4)
            for k in range(n_slab):
                hp_ref[b * ATT_BLK:(b + 1) * ATT_BLK, k * LANES:(k + 1) * LANES] = (
                    hs_ref[k, pl.ds(start, ATT_BLK, stride=4), :].astype(_BF16))

    def permute16():
        quarter = TILE // 4
        for p0 in range(4):
            for k in range(n_slab):
                hq_ref[k] = hs_ref[k, pl.ds(p0, quarter, stride=4), :]
            for p1 in range(4):
                lo = (p0 * 4 + p1) * ATT_BLK
                for k in range(n_slab):
                    hp_ref[lo:lo + ATT_BLK, k * LANES:(k + 1) * LANES] = (
                        hq_ref[k, pl.ds(p1, ATT_BLK, stride=4), :].astype(_BF16))

    def table_rows(t_ref, c, bi, r):
        if r == 1:
            return t_ref[c * SUB + bi * ATT_BLK:c * SUB + (bi + 1) * ATT_BLK, :]
        if r == 4:
            return t_ref[pl.ds(c * SUB + bi, ATT_BLK, stride=4), :]
        return t_ref[pl.ds(_res16(c, bi), ATT_BLK, stride=16), :]

    lane = lax.broadcasted_iota(jnp.int32, (ATT_BLK, LANES), 1)

    def run(lhs_ref, r):
        for c in range(TILE // SUB):
            lhs = lhs_ref[c * SUB:(c + 1) * SUB, :]
            o_ref[c * SUB:(c + 1) * SUB, ATT_V0:ATT_V0 + ATT_GW] = (
                _dot(lhs, wv_ref[...]).astype(_BF16))
            for col0, w_ref in ((ATT_Q0, wq_ref), (ATT_K0, wk_ref)):
                acc = _dot(lhs, w_ref[...])
                for bi in range(SUB // ATT_BLK):
                    lo = c * SUB + bi * ATT_BLK
                    cos = table_rows(ca_ref, c, bi, r)
                    sin = table_rows(sa_ref, c, bi, r)
                    outs = []
                    for hh in range(ATT_HPG):
                        x = acc[bi * ATT_BLK:(bi + 1) * ATT_BLK, hh * ATT_DH:(hh + 1) * ATT_DH]
                        partner = jnp.where(lane < ATT_ROT_HALF,
                                            pltpu.roll(x, LANES - ATT_ROT_HALF, 1),
                                            pltpu.roll(x, ATT_ROT_HALF, 1))
                        outs.append(x * cos + partner * sin)
                    o_ref[lo:lo + ATT_BLK, col0:col0 + ATT_GW] = (
                        jnp.concatenate(outs, axis=-1).astype(_BF16))

    @pl.when(g == 0)
    def _():
        build_slabs()
        run(h_ref, 1)

    @pl.when(g == 1)
    def _():
        permute4()
        run(hp_ref, 4)

    @pl.when(g == 2)
    def _():
        permute16()
        run(hp_ref, 16)


def _proj_att(h, w_in, layer, cos_a, sin_a):
    t = h.shape[0]
    n_g = len(ATT_GROUPS)
    tiles_per_seq = cos_a.shape[0] // TILE
    tab_spec = pl.BlockSpec((TILE, LANES), lambda i, g: (i % tiles_per_seq, 0))
    att_blk = ATT_COL_LO // ATT_GW

    def w_spec(which):
        return pl.BlockSpec((None, D_MODEL, ATT_GW),
                            lambda i, g: (layer, 0, att_blk + which * n_g + g))

    return pl.pallas_call(
        _proj_att_kernel,
        grid=(t // TILE, n_g),
        in_specs=[
            pl.BlockSpec((TILE, D_MODEL), lambda i, g: (i, 0)),
            w_spec(0), w_spec(1), w_spec(2),
            tab_spec, tab_spec,
        ],
        out_specs=pl.BlockSpec((None, TILE, 3 * ATT_GW), lambda i, g: (g, i, 0)),
        out_shape=jax.ShapeDtypeStruct((n_g, t, 3 * ATT_GW), _BF16),
        scratch_shapes=[
            pltpu.VMEM((D_MODEL // LANES, TILE, LANES), _F32),
            pltpu.VMEM((D_MODEL // LANES, TILE // 4, LANES), _F32),
            pltpu.VMEM((TILE, D_MODEL), _BF16),
        ],
        compiler_params=pltpu.CompilerParams(
            dimension_semantics=("arbitrary", "arbitrary"),
            vmem_limit_bytes=VMEM_LIMIT),
        name="proj_att",
    )(h, w_in, w_in, w_in, cos_a, sin_a)


def _dilated_kernel(cur0, prev0, cur1, prev1, cur2, prev2, o_ref, out_ref, lse_ref):
    i = pl.program_id(1)
    ss = pl.program_id(2)
    n_sub = TILE // SUB
    scale = 1.0 / math.sqrt(ATT_DH)
    ci = lax.broadcasted_iota(jnp.int32, (ATT_BLK, ATT_BLK), 0)
    mi = lax.broadcasted_iota(jnp.int32, (ATT_BLK, ATT_BLK), 1)
    prev_band = mi >= ci
    cur_band = mi <= ci

    groups = ((cur0, prev0, 1), (cur1, prev1, 4), (cur2, prev2, 16))

    def item(g, bi):
        cur_ref, prev_ref, r = groups[g]
        rows = slice(bi * ATT_BLK, (bi + 1) * ATT_BLK)
        if r == 1:
            if bi == 0:
                p_ref, prows = prev_ref, slice(0, ATT_BLK)
                has_prev = (i > 0) | (ss > 0)
            else:
                p_ref, prows = cur_ref, slice((bi - 1) * ATT_BLK, bi * ATT_BLK)
                has_prev = None
            dst = pl.ds(pl.multiple_of(ss * SUB + bi * ATT_BLK, ATT_BLK), ATT_BLK)
        elif r == 4:
            p_ref, prows = prev_ref, rows
            has_prev = (i > 0) | (ss > 0)
            dst = pl.ds(ss * SUB + bi, ATT_BLK, stride=4)
        else:
            p_ref, prows = prev_ref, rows
            has_prev = i > 0
            dst = pl.ds(_res16(ss, bi), ATT_BLK, stride=16)
        pmask = prev_band if has_prev is None else (prev_band & has_prev)
        return dict(g=g, cur_ref=cur_ref, p_ref=p_ref, rows=rows, prows=prows,
                    pmask=pmask, dst=dst)

    def head_cols(col0, h):
        return slice(col0 + h * ATT_DH, col0 + (h + 1) * ATT_DH)

    def score_stage(it):
        out = []
        for h in range(ATT_HPG):
            q = it["cur_ref"][it["rows"], head_cols(ATT_Q0, h)]
            out.append((_dot_nt(q, it["p_ref"][it["prows"], head_cols(ATT_K0, h)]),
                        _dot_nt(q, it["cur_ref"][it["rows"], head_cols(ATT_K0, h)])))
        return out

    def softmax_stage(it, scores):
        out = []
        for sp, sc in scores:
            sp = jnp.where(it["pmask"], sp * scale, NEG)
            sc = jnp.where(cur_band, sc * scale, NEG)
            m = jnp.max(jnp.maximum(sp, sc), axis=-1, keepdims=True)
            pp = jnp.exp(sp - m)
            pc = jnp.exp(sc - m)
            l = jnp.sum(pp + pc, axis=-1, keepdims=True)
            out.append((pp.astype(_BF16), pc.astype(_BF16), m, l))
        return out

    def value_stage(it, probs):
        for h, (pp, pc, m, l) in enumerate(probs):
            num = (_dot(pp, it["p_ref"][it["prows"], head_cols(ATT_V0, h)])
                   + _dot(pc, it["cur_ref"][it["rows"], head_cols(ATT_V0, h)]))
            out_ref[it["g"], h, it["dst"], :] = num * (1.0 / l)
            lse_ref[it["g"], h, it["dst"], :] = jnp.broadcast_to(m + jnp.log(l),
                                                                 (ATT_BLK, LANES))

    items = [item(g, bi) for g in range(len(groups)) for bi in range(SUB // ATT_BLK)]
    scores = score_stage(items[0])
    for n, it in enumerate(items):
        nxt = score_stage(items[n + 1]) if n + 1 < len(items) else None
        value_stage(it, softmax_stage(it, scores))
        scores = nxt

    @pl.when(ss == n_sub - 1)
    def _():
        n_g = len(ATT_GROUPS)

        def body(c, carry):
            rows = pl.ds(pl.multiple_of(c * ATT_BLK, ATT_BLK), ATT_BLK)
            outs = []
            for h in range(ATT_HPG):
                lses = [lse_ref[g, h, rows, :] for g in range(n_g)]
                top = functools.reduce(jnp.maximum, lses)
                ws = [jnp.exp(x - top) for x in lses]
                inv = 1.0 / sum(ws)
                outs.append(sum(ws[g] * out_ref[g, h, rows, :] for g in range(n_g)) * inv)
            o_ref[rows, :] = jnp.concatenate(outs, axis=-1).astype(_BF16)
            return carry
        lax.fori_loop(0, TILE // ATT_BLK, body, 0)


def _dilated(att, batch, seq):
    tiles = seq // TILE
    n_sub = TILE // SUB
    subs_per_seq = seq // SUB
    per = SUB // ATT_BLK

    def cur(g):
        return pl.BlockSpec(
            (None, SUB, 3 * ATT_GW),
            lambda b, i, s: (g, b * subs_per_seq + i * n_sub + s, 0))

    def prev_sub(g, back):
        return pl.BlockSpec(
            (None, SUB, 2 * ATT_GW),
            lambda b, i, s: (g, b * subs_per_seq + jnp.maximum(i * n_sub + s - back, 0), 0))

    def prev_blk(g):
        return pl.BlockSpec(
            (None, ATT_BLK, 2 * ATT_GW),
            lambda b, i, s: (g, b * subs_per_seq * per
                             + jnp.maximum((i * n_sub + s) * per - 1, 0), 0))

    in_specs = [cur(0), prev_blk(0), cur(1), prev_sub(1, 1), cur(2), prev_sub(2, n_sub)]

    return pl.pallas_call(
        _dilated_kernel,
        grid=(batch, tiles, n_sub),
        in_specs=in_specs,
        out_specs=pl.BlockSpec((TILE, ATT_GW), lambda b, i, s: (b * tiles + i, 0)),
        out_shape=jax.ShapeDtypeStruct((batch * seq, ATT_GW), _BF16),
        scratch_shapes=[
            pltpu.VMEM((len(ATT_GROUPS), ATT_HPG, TILE, LANES), _F32),
            pltpu.VMEM((len(ATT_GROUPS), ATT_HPG, TILE, LANES), _F32),
        ],
        compiler_params=pltpu.CompilerParams(
            dimension_semantics=("arbitrary", "arbitrary", "arbitrary"),
            vmem_limit_bytes=VMEM_LIMIT),
        name="dilated",
    )(*([att] * 6))


def _ret_merge_kernel(q_ref, k_ref, v_ref, g_ref, dm_ref, xi_ref, ze_ref, dec_ref,
                      x_ref, ya_ref, gr_ref, ga_ref, wr_ref, wa_ref, wo_ref, gn_ref,
                      o_ref, r_ref, yr_prev, yr_cur, mg_ref, m_ref, *, n_chunks, blocks_per_seq):
    s = pl.program_id(0)

    @pl.when(s % blocks_per_seq == 0)
    def _():
        r_ref[...] = jnp.zeros_like(r_ref)

    @pl.when(s == 0)
    def _():
        yr_prev[...] = jnp.zeros_like(yr_prev)

    pw = D_MODEL // MERGE_PIECES

    def branch_piece(p):
        cols = slice(p * pw, (p + 1) * pw)
        y_ret = _dot(yr_prev[...], wr_ref[:, cols])
        y_att = _dot(ya_ref[...], wa_ref[:, cols])
        mg_ref[:, cols] = (jax.nn.sigmoid(gr_ref[:, cols].astype(_F32)) * y_ret
                           + jax.nn.sigmoid(ga_ref[:, cols].astype(_F32)) * y_att
                           ).astype(_BF16)

    def out_piece(p):
        cols = slice(p * pw, (p + 1) * pw)
        m_ref[:, cols] = _dot(mg_ref[...], wo_ref[:, cols])

    def retention_unit(c, h):
        rows = slice(c * RET_C, (c + 1) * RET_C)
        q = q_ref[rows, h * RET_DK:(h + 1) * RET_DK]
        k = k_ref[rows, h * RET_DK:(h + 1) * RET_DK]
        v = v_ref[rows, h * RET_DV:(h + 1) * RET_DV]
        scores = _dot_nt(q, k) * dm_ref[h]
        qx = (q.astype(_F32) * xi_ref[h]).astype(_BF16)
        r_old = r_ref[h]
        y = _dot(jnp.concatenate([scores.astype(_BF16), qx], axis=1),
                 jnp.concatenate([v, r_old.astype(_BF16)], axis=0))
        kz = (k.astype(_F32) * ze_ref[h]).astype(_BF16)
        r_ref[h] = dec_ref[h] * r_old + _dot_tn(kz, v)
        mu = jnp.mean(y, axis=-1, keepdims=True)
        yc = y - mu
        var = jnp.mean(yc * yc, axis=-1, keepdims=True)
        yn = yc * lax.rsqrt(var + NORM_EPS)
        gate = g_ref[rows, h * RET_DV:(h + 1) * RET_DV].astype(_F32)
        yr_cur[rows, h * RET_DV:(h + 1) * RET_DV] = (
            gate * jax.nn.sigmoid(gate) * yn).astype(_BF16)

    units = [(c, h) for c in range(n_chunks) for h in range(RET_HEADS)]
    pieces = ([functools.partial(branch_piece, p) for p in range(MERGE_PIECES)]
              + [functools.partial(out_piece, p) for p in range(MERGE_PIECES)])
    assert len(units) == len(pieces)
    for (c, h), piece in zip(units, pieces):
        retention_unit(c, h)
        piece()
    o_ref[...] = x_ref[...] + _rms(m_ref[...], gn_ref[...])
    yr_prev[...] = yr_cur[...]


def _resident(stacked, layer):
    shape = stacked.shape[1:]
    return pl.BlockSpec((None,) + shape, lambda i: (layer,) + (0,) * len(shape),
                        pipeline_mode=pl.Buffered(1))


def _ret_merge(x, proj, ya, consts, wr, wa, wo, g, layer, seq, tc=512):
    t = x.shape[0]
    n_blocks = t // tc
    dm, xi, ze, dec = consts
    qk_w = RET_HEADS * RET_DK
    v_w = RET_HEADS * RET_DV
    colb = MAIN_COLB

    def ret_blk(width, col):
        return pl.BlockSpec((tc, width), lambda s: (jnp.minimum(s, n_blocks - 1), col))

    def merge_blk(width, col):
        return pl.BlockSpec((tc, width), lambda s: (jnp.maximum(s - 1, 0), col))

    def const_spec(a):
        return pl.BlockSpec(a.shape, lambda s: (0,) * a.ndim)

    return pl.pallas_call(
        functools.partial(_ret_merge_kernel, n_chunks=tc // RET_C,
                          blocks_per_seq=seq // tc),
        grid=(n_blocks + 1,),
        in_specs=[
            ret_blk(qk_w, _RQ * colb // qk_w), ret_blk(qk_w, _RK * colb // qk_w),
            ret_blk(v_w, _RV * colb // v_w), ret_blk(v_w, _RG * colb // v_w),
            const_spec(dm), const_spec(xi), const_spec(ze), const_spec(dec),
            merge_blk(D_MODEL, 0), merge_blk(ya.shape[1], 0),
            merge_blk(D_MODEL, _GR * colb // D_MODEL), merge_blk(D_MODEL, _GA * colb // D_MODEL),
            _resident(wr, layer), _resident(wa, layer), _resident(wo, layer),
            _resident(g, layer),
        ],
        out_specs=merge_blk(D_MODEL, 0),
        out_shape=jax.ShapeDtypeStruct((t, D_MODEL), _F32),
        scratch_shapes=[
            pltpu.VMEM((RET_HEADS, RET_DK, RET_DV), _F32),
            pltpu.VMEM((tc, v_w), _BF16),
            pltpu.VMEM((tc, v_w), _BF16),
            pltpu.VMEM((tc, D_MODEL), _BF16),
            pltpu.VMEM((tc, D_MODEL), _F32),
        ],
        compiler_params=pltpu.CompilerParams(
            dimension_semantics=("arbitrary",), vmem_limit_bytes=VMEM_LIMIT),
        name="ret_merge",
    )(proj, proj, proj, proj, dm, xi, ze, dec, x, ya, proj, proj, wr, wa, wo, g)


def _ffn_kernel(x_ref, gpre_ref, wg_ref, wu_ref, wd_ref, gpost_ref, o_ref, a_ref, *, bounds):
    x = x_ref[...]
    h = _rms(x, gpre_ref[...]).astype(_BF16)
    for lo, hi in zip(bounds[:-1], bounds[1:]):
        cols = slice(lo, hi)
        gate = _dot(h, wg_ref[:, cols])
        up = _dot(h, wu_ref[:, cols])
        a_ref[:, cols] = (gate * jax.nn.sigmoid(gate) * up).astype(_BF16)
    f = _dot(a_ref[...], wd_ref[...])
    o_ref[...] = x + _rms(f, gpost_ref[...])


def _ffn(x, gpre, wg, wu, wd, gpost, layer, tm=1024):
    t = x.shape[0]
    bounds = (0, 3 * MXU_W, 6 * MXU_W, 9 * MXU_W, D_FF)
    return pl.pallas_call(
        functools.partial(_ffn_kernel, bounds=bounds),
        grid=(t // tm,),
        in_specs=[
            pl.BlockSpec((tm, D_MODEL), lambda i: (i, 0)),
            _resident(gpre, layer), _resident(wg, layer), _resident(wu, layer),
            _resident(wd, layer), _resident(gpost, layer),
        ],
        out_specs=pl.BlockSpec((tm, D_MODEL), lambda i: (i, 0)),
        out_shape=jax.ShapeDtypeStruct((t, D_MODEL), _F32),
        scratch_shapes=[pltpu.VMEM((tm, D_FF), _BF16)],
        compiler_params=pltpu.CompilerParams(
            dimension_semantics=("arbitrary",), vmem_limit_bytes=VMEM_LIMIT),
        name="ffn",
    )(x, gpre, wg, wu, wd, gpost)


def _cos_sin(seq, inv_freq, coarse=64):
    hi = (jnp.arange(seq // coarse, dtype=_F32) * coarse)[:, None] * inv_freq[None, :]
    lo = jnp.arange(coarse, dtype=_F32)[:, None] * inv_freq[None, :]
    ch, sh = jnp.cos(hi)[:, None, :], jnp.sin(hi)[:, None, :]
    cl, sl = jnp.cos(lo)[None, :, :], jnp.sin(lo)[None, :, :]
    n = inv_freq.shape[0]
    return (ch * cl - sh * sl).reshape(seq, n), (sh * cl + ch * sl).reshape(seq, n)


def _rotary_tables(seq):
    half = RET_DK // 2
    cos_r, sin_r = _cos_sin(seq, RET_THETA ** (-jnp.arange(half, dtype=_F32) / half))
    cos, sin = _cos_sin(
        seq, ATT_THETA ** (-jnp.arange(ATT_ROT_HALF, dtype=_F32) / ATT_ROT_HALF))
    rest = LANES - 2 * ATT_ROT_HALF
    cos_a = jnp.concatenate([cos, cos, jnp.ones((seq, rest), _F32)], axis=-1)
    sin_a = jnp.concatenate([-sin, sin, jnp.zeros((seq, rest), _F32)], axis=-1)
    return cos_r, sin_r, cos_a, sin_a


def _retention_consts():
    c = RET_C
    lg = jnp.log(1.0 - 2.0 ** (-5.0 - jnp.arange(RET_HEADS, dtype=_F32)))
    pos = jnp.arange(c, dtype=_F32)
    rel = pos[:, None] - pos[None, :]
    dmask = jnp.where(rel[None] >= 0,
                      jnp.exp(jnp.maximum(rel, 0.0)[None] * lg[:, None, None]), 0.0)
    xi = jnp.exp((pos + 1.0)[None, :] * lg[:, None])[:, :, None]
    zeta = jnp.exp((c - 1.0 - pos)[None, :] * lg[:, None])[:, :, None]
    decay = jnp.broadcast_to(jnp.exp(c * lg)[:, None, None], (RET_HEADS, 1, RET_DV))
    return dmask, xi, zeta, decay


def kernel(x, w_in, w_ret_out, w_att_out, w_o, w_ffn_gate, w_ffn_up, w_ffn_down,
           g_pre_mix, g_post_mix, g_pre_ffn, g_post_ffn):
    batch, seq, d = x.shape
    depth = w_in.shape[0]
    assert d == D_MODEL and seq % TILE == 0
    cos_r, sin_r, cos_a, sin_a = _rotary_tables(seq)
    consts = _retention_consts()
    xf = x.reshape(batch * seq, d)
    w_in, w_ret_out, w_att_out, w_o, w_ffn_gate, w_ffn_up, w_ffn_down = (
        w.astype(_BF16) for w in (w_in, w_ret_out, w_att_out, w_o,
                                  w_ffn_gate, w_ffn_up, w_ffn_down))
    g_pre_mix, g_post_mix, g_pre_ffn, g_post_ffn = (
        g[:, None, :] for g in (g_pre_mix, g_post_mix, g_pre_ffn, g_post_ffn))
    for l in range(depth):
        proj, h = _proj_main(xf, g_pre_mix, w_in, l, cos_r, sin_r)
        att = _proj_att(h, w_in, l, cos_a, sin_a)
        ya = _dilated(att, batch, seq)
        xf = _ret_merge(xf, proj, ya, consts, w_ret_out, w_att_out, w_o, g_post_mix, l, seq)
        xf = _ffn(xf, g_pre_ffn, w_ffn_gate, w_ffn_up, w_ffn_down, g_post_ffn, l)
    return xf.reshape(batch, seq, d)
```

```python
import functools
import math

import jax
import jax.numpy as jnp
from jax import lax
from jax.experimental import pallas as pl
from jax.experimental.pallas import tpu as pltpu

D_MODEL = 1024
RET_HEADS = 4
RET_DK = 256
RET_DV = 512
RET_THETA = 10000.0
ATT_GROUPS = ((128, 1), (512, 4), (2048, 16))
ATT_HPG = 4
ATT_DH = 128
ATT_ROT_HALF = 16
ATT_THETA = 500000.0
D_FF = 2816
NORM_EPS = 1e-6
NEG = -1e30

LANES = 128
MXU_W = 256
ATT_BLK = 128
TILE = 2048
SUB = 512
RET_C = 256
MERGE_PIECES = 4
VMEM_LIMIT = 60 * 1024 * 1024

MAIN_W = 8192
MAIN_COLB = 1024
_RQ, _RK, _RV, _RG, _GR, _GA = 0, 1, 2, 4, 6, 7
ATT_GW = ATT_HPG * ATT_DH
ATT_K0, ATT_V0, ATT_Q0 = 0, ATT_GW, 2 * ATT_GW
ATT_COL_LO = 2 * RET_HEADS * RET_DK + 2 * RET_HEADS * RET_DV
GATE_COL_LO = ATT_COL_LO + 3 * len(ATT_GROUPS) * ATT_GW

_F32 = jnp.float32
_BF16 = jnp.bfloat16


def _dot(a, b):
    return jnp.dot(a, b, preferred_element_type=_F32)


def _dot_nt(a, b):
    return lax.dot_general(a, b, (((1,), (1,)), ((), ())), preferred_element_type=_F32)


def _dot_tn(a, b):
    return lax.dot_general(a, b, (((0,), (0,)), ((), ())), preferred_element_type=_F32)


def _res16(sub, blk):
    return 4 * blk + sub


def _rms(x, g):
    return x * lax.rsqrt(jnp.mean(x * x, axis=-1, keepdims=True) + NORM_EPS) * g


def _proj_main_kernel(x0_ref, x1_ref, x2_ref, x3_ref, g_ref, w_ref, wlo_ref, whi_ref,
                      cr_ref, sr_ref, o_ref, h_ref):
    j = pl.program_id(1)
    x_refs = (x0_ref, x1_ref, x2_ref, x3_ref)

    def norm_chunk(c):
        for r in range(0, SUB, 256):
            h_ref[c * SUB + r:c * SUB + r + 256, :] = _rms(
                x_refs[c][r:r + 256, :], g_ref[...]).astype(_BF16)

    def run(epilogue, prologue=None):
        for c in range(TILE // SUB):
            if prologue is not None:
                prologue(c)
            epilogue(_dot(h_ref[c * SUB:(c + 1) * SUB, :], w_ref[...]), c)

    def plain(acc, c):
        o_ref[c * SUB:(c + 1) * SUB, :] = acc.astype(_BF16)

    def run_halves():
        half = MAIN_COLB // 2
        for c in range(TILE // SUB):
            lhs = h_ref[c * SUB:(c + 1) * SUB, :]
            o_ref[c * SUB:(c + 1) * SUB, :half] = _dot(lhs, wlo_ref[...]).astype(_BF16)
            o_ref[c * SUB:(c + 1) * SUB, half:] = _dot(lhs, whi_ref[...]).astype(_BF16)

    def ret_rotary(scale):
        def f(acc, c):
            for bi in range(SUB // ATT_BLK):
                lo = c * SUB + bi * ATT_BLK
                cos = cr_ref[lo:lo + ATT_BLK, :]
                sin = sr_ref[lo:lo + ATT_BLK, :]
                a = acc[bi * ATT_BLK:(bi + 1) * ATT_BLK, :]
                outs = []
                for hh in range(MAIN_COLB // RET_DK):
                    x1 = a[:, hh * RET_DK:hh * RET_DK + LANES]
                    x2 = a[:, hh * RET_DK + LANES:(hh + 1) * RET_DK]
                    outs.append((x1 * cos - x2 * sin) * scale)
                    outs.append((x1 * sin + x2 * cos) * scale)
                o_ref[lo:lo + ATT_BLK, :] = jnp.concatenate(outs, axis=-1).astype(_BF16)
        return f

    @pl.when(j == _RQ)
    def _():
        run(ret_rotary(1.0), prologue=norm_chunk)

    @pl.when(j == _RK)
    def _():
        run(ret_rotary(RET_DK ** -0.5))

    @pl.when((j >= _RV) & (j < _GR))
    def _():
        run(plain)

    @pl.when(j >= _GR)
    def _():
        run_halves()


def _proj_main(x, g, w_in, layer, cos_r, sin_r):
    t = x.shape[0]
    n_tiles = t // TILE
    n_col = MAIN_W // MAIN_COLB
    tiles_per_seq = cos_r.shape[0] // TILE
    half = MAIN_COLB // 2
    gate_blk = GATE_COL_LO // half

    def tile_from(i, j, switch_at):
        return jnp.minimum(i + (j >= switch_at).astype(jnp.int32), n_tiles - 1)

    def x_chunk(c):
        per = TILE // SUB
        return pl.BlockSpec(
            (SUB, D_MODEL), lambda i, j: (tile_from(i, j, n_col - per + c) * per + c, 0))

    tab_spec = pl.BlockSpec(
        (TILE, LANES), lambda i, j: (tile_from(i, j, _RV) % tiles_per_seq, 0))

    def gate_half(which):
        return pl.BlockSpec(
            (None, D_MODEL, half),
            lambda i, j: (layer, 0, gate_blk + 2 * jnp.maximum(j - _GR, 0) + which))

    return pl.pallas_call(
        _proj_main_kernel,
        grid=(n_tiles, n_col),
        in_specs=[
            x_chunk(0), x_chunk(1), x_chunk(2), x_chunk(3),
            pl.BlockSpec((None, 1, D_MODEL), lambda i, j: (layer, 0, 0)),
            pl.BlockSpec((None, D_MODEL, MAIN_COLB),
                         lambda i, j: (layer, 0, jnp.minimum(j, _GR - 1))),
            gate_half(0), gate_half(1),
            tab_spec, tab_spec,
        ],
        out_specs=[
            pl.BlockSpec((TILE, MAIN_COLB), lambda i, j: (i, j)),
            pl.BlockSpec((TILE, D_MODEL), lambda i, j: (i, 0)),
        ],
        out_shape=[
            jax.ShapeDtypeStruct((t, MAIN_W), _BF16),
            jax.ShapeDtypeStruct((t, D_MODEL), _BF16),
        ],
        compiler_params=pltpu.CompilerParams(
            dimension_semantics=("arbitrary", "arbitrary"),
            vmem_limit_bytes=VMEM_LIMIT),
        name="proj_main",
    )(x, x, x, x, g, w_in, w_in, w_in, cos_r, sin_r)


def _proj_att_kernel(h_ref, wq_ref, wk_ref, wv_ref, ca_ref, sa_ref, o_ref,
                     hs_ref, hq_ref, hp_ref):
    g = pl.program_id(1)
    n_slab = D_MODEL // LANES
    n_blk = TILE // ATT_BLK

    def build_slabs():
        for c in range(TILE // 256):
            y = h_ref[c * 256:(c + 1) * 256, :].astype(_F32)
            for k in range(n_slab):
                hs_ref[k, c * 256:(c + 1) * 256, :] = y[:, k * LANES:(k + 1) * LANES]

    def permute4():
        for b in range(n_blk):
            start = (b // 4) * (4 * ATT_BLK) + (b % 4)
            for k in range(n_slab):
                hp_ref[b * ATT_BLK:(b + 1) * ATT_BLK, k * LANES:(k + 1) * LANES] = (
                    hs_ref[k, pl.ds(start, ATT_BLK, stride=4), :].astype(_BF16))

    def permute16():
        quarter = TILE // 4
        for p0 in range(4):
            for k in range(n_slab):
                hq_ref[k] = hs_ref[k, pl.ds(p0, quarter, stride=4), :]
            for p1 in range(4):
                lo = (p0 * 4 + p1) * ATT_BLK
                for k in range(n_slab):
                    hp_ref[lo:lo + ATT_BLK, k * LANES:(k + 1) * LANES] = (
                        hq_ref[k, pl.ds(p1, ATT_BLK, stride=4), :].astype(_BF16))

    def table_rows(t_ref, c, bi, r):
        if r == 1:
            return t_ref[c * SUB + bi * ATT_BLK:c * SUB + (bi + 1) * ATT_BLK, :]
        if r == 4:
            return t_ref[pl.ds(c * SUB + bi, ATT_BLK, stride=4), :]
        return t_ref[pl.ds(_res16(c, bi), ATT_BLK, stride=16), :]

    lane = lax.broadcasted_iota(jnp.int32, (ATT_BLK, LANES), 1)

    def run(lhs_ref, r):
        for c in range(TILE // SUB):
            lhs = lhs_ref[c * SUB:(c + 1) * SUB, :]
            o_ref[c * SUB:(c + 1) * SUB, ATT_V0:ATT_V0 + ATT_GW] = (
                _dot(lhs, wv_ref[...]).astype(_BF16))
            for col0, w_ref in ((ATT_Q0, wq_ref), (ATT_K0, wk_ref)):
                acc = _dot(lhs, w_ref[...])
                for bi in range(SUB // ATT_BLK):
                    lo = c * SUB + bi * ATT_BLK
                    cos = table_rows(ca_ref, c, bi, r)
                    sin = table_rows(sa_ref, c, bi, r)
                    outs = []
                    for hh in range(ATT_HPG):
                        x = acc[bi * ATT_BLK:(bi + 1) * ATT_BLK, hh * ATT_DH:(hh + 1) * ATT_DH]
                        partner = jnp.where(lane < ATT_ROT_HALF,
                                            pltpu.roll(x, LANES - ATT_ROT_HALF, 1),
                                            pltpu.roll(x, ATT_ROT_HALF, 1))
                        outs.append(x * cos + partner * sin)
                    o_ref[lo:lo + ATT_BLK, col0:col0 + ATT_GW] = (
                        jnp.concatenate(outs, axis=-1).astype(_BF16))

    @pl.when(g == 0)
    def _():
        build_slabs()
        run(h_ref, 1)

    @pl.when(g == 1)
    def _():
        permute4()
        run(hp_ref, 4)

    @pl.when(g == 2)
    def _():
        permute16()
        run(hp_ref, 16)


def _proj_att(h, w_in, layer, cos_a, sin_a):
    t = h.shape[0]
    n_g = len(ATT_GROUPS)
    tiles_per_seq = cos_a.shape[0] // TILE
    tab_spec = pl.BlockSpec((TILE, LANES), lambda i, g: (i % tiles_per_seq, 0))
    att_blk = ATT_COL_LO // ATT_GW

    def w_spec(which):
        return pl.BlockSpec((None, D_MODEL, ATT_GW),
                            lambda i, g: (layer, 0, att_blk + which * n_g + g))

    return pl.pallas_call(
        _proj_att_kernel,
        grid=(t // TILE, n_g),
        in_specs=[
            pl.BlockSpec((TILE, D_MODEL), lambda i, g: (
                jnp.minimum(i + (g >= 1).astype(jnp.int32), t // TILE - 1), 0)),
            w_spec(0), w_spec(1), w_spec(2),
            tab_spec, tab_spec,
        ],
        out_specs=pl.BlockSpec((None, TILE, 3 * ATT_GW), lambda i, g: (g, i, 0)),
        out_shape=jax.ShapeDtypeStruct((n_g, t, 3 * ATT_GW), _BF16),
        scratch_shapes=[
            pltpu.VMEM((D_MODEL // LANES, TILE, LANES), _F32),
            pltpu.VMEM((D_MODEL // LANES, TILE // 4, LANES), _F32),
            pltpu.VMEM((TILE, D_MODEL), _BF16),
        ],
        compiler_params=pltpu.CompilerParams(
            dimension_semantics=("arbitrary", "arbitrary"),
            vmem_limit_bytes=VMEM_LIMIT),
        name="proj_att",
    )(h, w_in, w_in, w_in, cos_a, sin_a)


def _dilated_kernel(cur0, prev0, cur1, prev1, cur2, prev2, o_ref, out_ref, lse_ref):
    i = pl.program_id(1)
    ss = pl.program_id(2)
    n_sub = TILE // SUB
    scale = 1.0 / math.sqrt(ATT_DH)
    ci = lax.broadcasted_iota(jnp.int32, (ATT_BLK, ATT_BLK), 0)
    mi = lax.broadcasted_iota(jnp.int32, (ATT_BLK, ATT_BLK), 1)
    prev_band = mi >= ci
    cur_band = mi <= ci

    groups = ((cur0, prev0, 1), (cur1, prev1, 4), (cur2, prev2, 16))

    def item(g, bi):
        cur_ref, prev_ref, r = groups[g]
        rows = slice(bi * ATT_BLK, (bi + 1) * ATT_BLK)
        if r == 1:
            if bi == 0:
                p_ref, prows = prev_ref, slice(0, ATT_BLK)
                has_prev = (i > 0) | (ss > 0)
            else:
                p_ref, prows = cur_ref, slice((bi - 1) * ATT_BLK, bi * ATT_BLK)
                has_prev = None
            dst = pl.ds(pl.multiple_of(ss * SUB + bi * ATT_BLK, ATT_BLK), ATT_BLK)
        elif r == 4:
            p_ref, prows = prev_ref, rows
            has_prev = (i > 0) | (ss > 0)
            dst = pl.ds(ss * SUB + bi, ATT_BLK, stride=4)
        else:
            p_ref, prows = prev_ref, rows
            has_prev = i > 0
            dst = pl.ds(_res16(ss, bi), ATT_BLK, stride=16)
        pmask = prev_band if has_prev is None else (prev_band & has_prev)
        return dict(g=g, cur_ref=cur_ref, p_ref=p_ref, rows=rows, prows=prows,
                    pmask=pmask, dst=dst)

    def head_cols(col0, h):
        return slice(col0 + h * ATT_DH, col0 + (h + 1) * ATT_DH)

    def score_stage(it):
        out = []
        for h in range(ATT_HPG):
            q = it["cur_ref"][it["rows"], head_cols(ATT_Q0, h)]
            out.append((_dot_nt(q, it["p_ref"][it["prows"], head_cols(ATT_K0, h)]),
                        _dot_nt(q, it["cur_ref"][it["rows"], head_cols(ATT_K0, h)])))
        return out

    def softmax_stage(it, scores):
        out = []
        for sp, sc in scores:
            sp = jnp.where(it["pmask"], sp * scale, NEG)
            sc = jnp.where(cur_band, sc * scale, NEG)
            m = jnp.max(jnp.maximum(sp, sc), axis=-1, keepdims=True)
            pp = jnp.exp(sp - m)
            pc = jnp.exp(sc - m)
            l = jnp.sum(pp + pc, axis=-1, keepdims=True)
            out.append((pp.astype(_BF16), pc.astype(_BF16), m, l))
        return out

    def value_stage(it, probs):
        for h, (pp, pc, m, l) in enumerate(probs):
            num = (_dot(pp, it["p_ref"][it["prows"], head_cols(ATT_V0, h)])
                   + _dot(pc, it["cur_ref"][it["rows"], head_cols(ATT_V0, h)]))
            out_ref[it["g"], h, it["dst"], :] = num * (1.0 / l)
            lse_ref[it["g"], h, it["dst"], :] = jnp.broadcast_to(m + jnp.log(l),
                                                                 (ATT_BLK, LANES))

    items = [item(g, bi) for g in range(len(groups)) for bi in range(SUB // ATT_BLK)]
    scores = score_stage(items[0])
    for n, it in enumerate(items):
        nxt = score_stage(items[n + 1]) if n + 1 < len(items) else None
        value_stage(it, softmax_stage(it, scores))
        scores = nxt

    @pl.when(ss == n_sub - 1)
    def _():
        n_g = len(ATT_GROUPS)

        def body(c, carry):
            rows = pl.ds(pl.multiple_of(c * ATT_BLK, ATT_BLK), ATT_BLK)
            outs = []
            for h in range(ATT_HPG):
                lses = [lse_ref[g, h, rows, :] for g in range(n_g)]
                top = functools.reduce(jnp.maximum, lses)
                ws = [jnp.exp(x - top) for x in lses]
                inv = 1.0 / sum(ws)
                outs.append(sum(ws[g] * out_ref[g, h, rows, :] for g in range(n_g)) * inv)
            o_ref[rows, :] = jnp.concatenate(outs, axis=-1).astype(_BF16)
            return carry
        lax.fori_loop(0, TILE // ATT_BLK, body, 0)


def _dilated(att, batch, seq):
    tiles = seq // TILE
    n_sub = TILE // SUB
    subs_per_seq = seq // SUB
    per = SUB // ATT_BLK

    def cur(g):
        return pl.BlockSpec(
            (None, SUB, 3 * ATT_GW),
            lambda b, i, s: (g, b * subs_per_seq + i * n_sub + s, 0))

    def prev_sub(g, back):
        return pl.BlockSpec(
            (None, SUB, 2 * ATT_GW),
            lambda b, i, s: (g, b * subs_per_seq + jnp.maximum(i * n_sub + s - back, 0), 0))

    def prev_blk(g):
        return pl.BlockSpec(
            (None, ATT_BLK, 2 * ATT_GW),
            lambda b, i, s: (g, b * subs_per_seq * per
                             + jnp.maximum((i * n_sub + s) * per - 1, 0), 0))

    in_specs = [cur(0), prev_blk(0), cur(1), prev_sub(1, 1), cur(2), prev_sub(2, n_sub)]

    return pl.pallas_call(
        _dilated_kernel,
        grid=(batch, tiles, n_sub),
        in_specs=in_specs,
        out_specs=pl.BlockSpec((TILE, ATT_GW), lambda b, i, s: (b * tiles + i, 0)),
        out_shape=jax.ShapeDtypeStruct((batch * seq, ATT_GW), _BF16),
        scratch_shapes=[
            pltpu.VMEM((len(ATT_GROUPS), ATT_HPG, TILE, LANES), _F32),
            pltpu.VMEM((len(ATT_GROUPS), ATT_HPG, TILE, LANES), _F32),
        ],
        compiler_params=pltpu.CompilerParams(
            dimension_semantics=("arbitrary", "arbitrary", "arbitrary"),
            vmem_limit_bytes=VMEM_LIMIT),
        name="dilated",
    )(*([att] * 6))


def _ret_merge_kernel(q_ref, k_ref, v_ref, g_ref, dm_ref, xi_ref, ze_ref, dec_ref,
                      x_ref, ya_ref, gr_ref, ga_ref, wr_ref, wa_ref, wo_ref, gn_ref,
                      o_ref, r_ref, yr_prev, yr_cur, mg_ref, m_ref, *, n_chunks, blocks_per_seq):
    s = pl.program_id(0)

    @pl.when(s % blocks_per_seq == 0)
    def _():
        r_ref[...] = jnp.zeros_like(r_ref)

    @pl.when(s == 0)
    def _():
        yr_prev[...] = jnp.zeros_like(yr_prev)

    pw = D_MODEL // MERGE_PIECES

    def branch_piece(p):
        cols = slice(p * pw, (p + 1) * pw)
        y_ret = _dot(yr_prev[...], wr_ref[:, cols])
        y_att = _dot(ya_ref[...], wa_ref[:, cols])
        mg_ref[:, cols] = (jax.nn.sigmoid(gr_ref[:, cols].astype(_F32)) * y_ret
                           + jax.nn.sigmoid(ga_ref[:, cols].astype(_F32)) * y_att
                           ).astype(_BF16)

    def out_piece(p):
        cols = slice(p * pw, (p + 1) * pw)
        m_ref[:, cols] = _dot(mg_ref[...], wo_ref[:, cols])

    def retention_unit(c, h):
        rows = slice(c * RET_C, (c + 1) * RET_C)
        q = q_ref[rows, h * RET_DK:(h + 1) * RET_DK]
        k = k_ref[rows, h * RET_DK:(h + 1) * RET_DK]
        v = v_ref[rows, h * RET_DV:(h + 1) * RET_DV]
        scores = _dot_nt(q, k) * dm_ref[h]
        qx = (q.astype(_F32) * xi_ref[h]).astype(_BF16)
        r_old = r_ref[h]
        y = _dot(jnp.concatenate([scores.astype(_BF16), qx], axis=1),
                 jnp.concatenate([v, r_old.astype(_BF16)], axis=0))
        kz = (k.astype(_F32) * ze_ref[h]).astype(_BF16)
        r_ref[h] = dec_ref[h] * r_old + _dot_tn(kz, v)
        mu = jnp.mean(y, axis=-1, keepdims=True)
        yc = y - mu
        var = jnp.mean(yc * yc, axis=-1, keepdims=True)
        yn = yc * lax.rsqrt(var + NORM_EPS)
        gate = g_ref[rows, h * RET_DV:(h + 1) * RET_DV].astype(_F32)
        yr_cur[rows, h * RET_DV:(h + 1) * RET_DV] = (
            gate * jax.nn.sigmoid(gate) * yn).astype(_BF16)

    units = [(c, h) for c in range(n_chunks) for h in range(RET_HEADS)]
    pieces = ([functools.partial(branch_piece, p) for p in range(MERGE_PIECES)]
              + [functools.partial(out_piece, p) for p in range(MERGE_PIECES)])
    assert len(units) == len(pieces)
    for (c, h), piece in zip(units, pieces):
        retention_unit(c, h)
        piece()
    o_ref[...] = x_ref[...] + _rms(m_ref[...], gn_ref[...])
    yr_prev[...] = yr_cur[...]


def _resident(stacked, layer):
    shape = stacked.shape[1:]
    return pl.BlockSpec((None,) + shape, lambda i: (layer,) + (0,) * len(shape),
                        pipeline_mode=pl.Buffered(1))


def _ret_merge(x, proj, ya, consts, wr, wa, wo, g, layer, seq, tc=512):
    t = x.shape[0]
    n_blocks = t // tc
    dm, xi, ze, dec = consts
    qk_w = RET_HEADS * RET_DK
    v_w = RET_HEADS * RET_DV
    colb = MAIN_COLB

    def ret_blk(width, col):
        return pl.BlockSpec((tc, width), lambda s: (jnp.minimum(s, n_blocks - 1), col))

    def merge_blk(width, col):
        return pl.BlockSpec((tc, width), lambda s: (jnp.maximum(s - 1, 0), col))

    def const_spec(a):
        return pl.BlockSpec(a.shape, lambda s: (0,) * a.ndim)

    return pl.pallas_call(
        functools.partial(_ret_merge_kernel, n_chunks=tc // RET_C,
                          blocks_per_seq=seq // tc),
        grid=(n_blocks + 1,),
        in_specs=[
            ret_blk(qk_w, _RQ * colb // qk_w), ret_blk(qk_w, _RK * colb // qk_w),
            ret_blk(v_w, _RV * colb // v_w), ret_blk(v_w, _RG * colb // v_w),
            const_spec(dm), const_spec(xi), const_spec(ze), const_spec(dec),
            merge_blk(D_MODEL, 0), merge_blk(ya.shape[1], 0),
            merge_blk(D_MODEL, _GR * colb // D_MODEL), merge_blk(D_MODEL, _GA * colb // D_MODEL),
            _resident(wr, layer), _resident(wa, layer), _resident(wo, layer),
            _resident(g, layer),
        ],
        out_specs=merge_blk(D_MODEL, 0),
        out_shape=jax.ShapeDtypeStruct((t, D_MODEL), _F32),
        scratch_shapes=[
            pltpu.VMEM((RET_HEADS, RET_DK, RET_DV), _F32),
            pltpu.VMEM((tc, v_w), _BF16),
            pltpu.VMEM((tc, v_w), _BF16),
            pltpu.VMEM((tc, D_MODEL), _BF16),
            pltpu.VMEM((tc, D_MODEL), _F32),
        ],
        compiler_params=pltpu.CompilerParams(
            dimension_semantics=("arbitrary",), vmem_limit_bytes=VMEM_LIMIT),
        name="ret_merge",
    )(proj, proj, proj, proj, dm, xi, ze, dec, x, ya, proj, proj, wr, wa, wo, g)


def _ffn_kernel(x_ref, gpre_ref, wg_ref, wu_ref, wd_ref, gpost_ref, o_ref, a_ref, *, bounds):
    x = x_ref[...]
    h = _rms(x, gpre_ref[...]).astype(_BF16)
    for lo, hi in zip(bounds[:-1], bounds[1:]):
        cols = slice(lo, hi)
        gate = _dot(h, wg_ref[:, cols])
        up = _dot(h, wu_ref[:, cols])
        a_ref[:, cols] = (gate * jax.nn.sigmoid(gate) * up).astype(_BF16)
    f = _dot(a_ref[...], wd_ref[...])
    o_ref[...] = x + _rms(f, gpost_ref[...])


def _ffn(x, gpre, wg, wu, wd, gpost, layer, tm=1024):
    t = x.shape[0]
    bounds = (0, 3 * MXU_W, 6 * MXU_W, 9 * MXU_W, D_FF)
    return pl.pallas_call(
        functools.partial(_ffn_kernel, bounds=bounds),
        grid=(t // tm,),
        in_specs=[
            pl.BlockSpec((tm, D_MODEL), lambda i: (i, 0)),
            _resident(gpre, layer), _resident(wg, layer), _resident(wu, layer),
            _resident(wd, layer), _resident(gpost, layer),
        ],
        out_specs=pl.BlockSpec((tm, D_MODEL), lambda i: (i, 0)),
        out_shape=jax.ShapeDtypeStruct((t, D_MODEL), _F32),
        scratch_shapes=[pltpu.VMEM((tm, D_FF), _BF16)],
        compiler_params=pltpu.CompilerParams(
            dimension_semantics=("arbitrary",), vmem_limit_bytes=VMEM_LIMIT),
        name="ffn",
    )(x, gpre, wg, wu, wd, gpost)


def _cos_sin(seq, inv_freq, coarse=64):
    hi = (jnp.arange(seq // coarse, dtype=_F32) * coarse)[:, None] * inv_freq[None, :]
    lo = jnp.arange(coarse, dtype=_F32)[:, None] * inv_freq[None, :]
    ch, sh = jnp.cos(hi)[:, None, :], jnp.sin(hi)[:, None, :]
    cl, sl = jnp.cos(lo)[None, :, :], jnp.sin(lo)[None, :, :]
    n = inv_freq.shape[0]
    return (ch * cl - sh * sl).reshape(seq, n), (sh * cl + ch * sl).reshape(seq, n)


def _rotary_tables(seq):
    half = RET_DK // 2
    cos_r, sin_r = _cos_sin(seq, RET_THETA ** (-jnp.arange(half, dtype=_F32) / half))
    cos, sin = _cos_sin(
        seq, ATT_THETA ** (-jnp.arange(ATT_ROT_HALF, dtype=_F32) / ATT_ROT_HALF))
    rest = LANES - 2 * ATT_ROT_HALF
    cos_a = jnp.concatenate([cos, cos, jnp.ones((seq, rest), _F32)], axis=-1)
    sin_a = jnp.concatenate([-sin, sin, jnp.zeros((seq, rest), _F32)], axis=-1)
    return cos_r, sin_r, cos_a, sin_a


def _retention_consts():
    c = RET_C
    lg = jnp.log(1.0 - 2.0 ** (-5.0 - jnp.arange(RET_HEADS, dtype=_F32)))
    pos = jnp.arange(c, dtype=_F32)
    rel = pos[:, None] - pos[None, :]
    dmask = jnp.where(rel[None] >= 0,
                      jnp.exp(jnp.maximum(rel, 0.0)[None] * lg[:, None, None]), 0.0)
    xi = jnp.exp((pos + 1.0)[None, :] * lg[:, None])[:, :, None]
    zeta = jnp.exp((c - 1.0 - pos)[None, :] * lg[:, None])[:, :, None]
    decay = jnp.broadcast_to(jnp.exp(c * lg)[:, None, None], (RET_HEADS, 1, RET_DV))
    return dmask, xi, zeta, decay


def kernel(x, w_in, w_ret_out, w_att_out, w_o, w_ffn_gate, w_ffn_up, w_ffn_down,
           g_pre_mix, g_post_mix, g_pre_ffn, g_post_ffn):
    batch, seq, d = x.shape
    depth = w_in.shape[0]
    assert d == D_MODEL and seq % TILE == 0
    cos_r, sin_r, cos_a, sin_a = _rotary_tables(seq)
    consts = _retention_consts()
    xf = x.reshape(batch * seq, d)
    w_in, w_ret_out, w_att_out, w_o, w_ffn_gate, w_ffn_up, w_ffn_down = (
        w.astype(_BF16) for w in (w_in, w_ret_out, w_att_out, w_o,
                                  w_ffn_gate, w_ffn_up, w_ffn_down))
    g_pre_mix, g_post_mix, g_pre_ffn, g_post_ffn = (
        g[:, None, :] for g in (g_pre_mix, g_post_mix, g_pre_ffn, g_post_ffn))
    for l in range(depth):
        proj, h = _proj_main(xf, g_pre_mix, w_in, l, cos_r, sin_r)
        att = _proj_att(h, w_in, l, cos_a, sin_a)
        ya = _dilated(att, batch, seq)
        xf = _ret_merge(xf, proj, ya, consts, w_ret_out, w_att_out, w_o, g_post_mix, l, seq)
        xf = _ffn(xf, g_pre_ffn, w_ffn_gate, w_ffn_up, w_ffn_down, g_post_ffn, l)
    return xf.reshape(batch, seq, d)
```

```python
import functools
import math

import jax
import jax.numpy as jnp
from jax import lax
from jax.experimental import pallas as pl
from jax.experimental.pallas import tpu as pltpu

D_MODEL = 1024
RET_HEADS = 4
RET_DK = 256
RET_DV = 512
RET_THETA = 10000.0
ATT_GROUPS = ((128, 1), (512, 4), (2048, 16))
ATT_HPG = 4
ATT_DH = 128
ATT_ROT_HALF = 16
ATT_THETA = 500000.0
D_FF = 2816
NORM_EPS = 1e-6
NEG = -1e30

LANES = 128
MXU_W = 256
ATT_BLK = 128
TILE = 2048
SUB = 512
RET_C = 256
MERGE_PIECES = 4
FFN_ROWS = 512
VMEM_LIMIT = 60 * 1024 * 1024

MAIN_W = 8192
MAIN_COLB = 1024
_RQ, _RK, _RV, _RG, _GR, _GA = 0, 1, 2, 4, 6, 7
ATT_GW = ATT_HPG * ATT_DH
ATT_K0, ATT_V0, ATT_Q0 = 0, ATT_GW, 2 * ATT_GW
ATT_COL_LO = 2 * RET_HEADS * RET_DK + 2 * RET_HEADS * RET_DV
GATE_COL_LO = ATT_COL_LO + 3 * len(ATT_GROUPS) * ATT_GW

_F32 = jnp.float32
_BF16 = jnp.bfloat16


def _dot(a, b):
    return jnp.dot(a, b, preferred_element_type=_F32)


def _dot_nt(a, b):
    return lax.dot_general(a, b, (((1,), (1,)), ((), ())), preferred_element_type=_F32)


def _dot_tn(a, b):
    return lax.dot_general(a, b, (((0,), (0,)), ((), ())), preferred_element_type=_F32)


def _res16(sub, blk):
    return 4 * blk + sub


def _rms(x, g):
    return x * lax.rsqrt(jnp.mean(x * x, axis=-1, keepdims=True) + NORM_EPS) * g


def _proj_main_kernel(x0_ref, x1_ref, x2_ref, x3_ref, g_ref, w_ref, wlo_ref, whi_ref,
                      cr_ref, sr_ref, o_ref, h_ref):
    j = pl.program_id(1)
    x_refs = (x0_ref, x1_ref, x2_ref, x3_ref)

    def norm_chunk(c):
        for r in range(0, SUB, 256):
            h_ref[c * SUB + r:c * SUB + r + 256, :] = _rms(
                x_refs[c][r:r + 256, :], g_ref[...]).astype(_BF16)

    def run(epilogue, prologue=None):
        for c in range(TILE // SUB):
            if prologue is not None:
                prologue(c)
            epilogue(_dot(h_ref[c * SUB:(c + 1) * SUB, :], w_ref[...]), c)

    def plain(acc, c):
        o_ref[c * SUB:(c + 1) * SUB, :] = acc.astype(_BF16)

    def run_halves():
        half = MAIN_COLB // 2
        for c in range(TILE // SUB):
            lhs = h_ref[c * SUB:(c + 1) * SUB, :]
            o_ref[c * SUB:(c + 1) * SUB, :half] = _dot(lhs, wlo_ref[...]).astype(_BF16)
            o_ref[c * SUB:(c + 1) * SUB, half:] = _dot(lhs, whi_ref[...]).astype(_BF16)

    def ret_rotary(scale):
        def f(acc, c):
            for bi in range(SUB // ATT_BLK):
                lo = c * SUB + bi * ATT_BLK
                cos = cr_ref[lo:lo + ATT_BLK, :]
                sin = sr_ref[lo:lo + ATT_BLK, :]
                a = acc[bi * ATT_BLK:(bi + 1) * ATT_BLK, :]
                outs = []
                for hh in range(MAIN_COLB // RET_DK):
                    x1 = a[:, hh * RET_DK:hh * RET_DK + LANES]
                    x2 = a[:, hh * RET_DK + LANES:(hh + 1) * RET_DK]
                    outs.append((x1 * cos - x2 * sin) * scale)
                    outs.append((x1 * sin + x2 * cos) * scale)
                o_ref[lo:lo + ATT_BLK, :] = jnp.concatenate(outs, axis=-1).astype(_BF16)
        return f

    @pl.when(j == _RQ)
    def _():
        run(ret_rotary(1.0), prologue=norm_chunk)

    @pl.when(j == _RK)
    def _():
        run(ret_rotary(RET_DK ** -0.5))

    @pl.when((j >= _RV) & (j < _GR))
    def _():
        run(plain)

    @pl.when(j >= _GR)
    def _():
        run_halves()


def _proj_main(x, g, w_in, layer, cos_r, sin_r):
    t = x.shape[0]
    n_tiles = t // TILE
    n_col = MAIN_W // MAIN_COLB
    tiles_per_seq = cos_r.shape[0] // TILE
    half = MAIN_COLB // 2
    gate_blk = GATE_COL_LO // half

    def tile_from(i, j, switch_at):
        return jnp.minimum(i + (j >= switch_at).astype(jnp.int32), n_tiles - 1)

    def x_chunk(c):
        per = TILE // SUB
        return pl.BlockSpec(
            (SUB, D_MODEL), lambda i, j: (tile_from(i, j, n_col - per + c) * per + c, 0))

    tab_spec = pl.BlockSpec(
        (TILE, LANES), lambda i, j: (tile_from(i, j, _RV) % tiles_per_seq, 0))

    def gate_half(which):
        return pl.BlockSpec(
            (None, D_MODEL, half),
            lambda i, j: (layer, 0, gate_blk + 2 * jnp.maximum(j - _GR, 0) + which))

    return pl.pallas_call(
        _proj_main_kernel,
        grid=(n_tiles, n_col),
        in_specs=[
            x_chunk(0), x_chunk(1), x_chunk(2), x_chunk(3),
            pl.BlockSpec((None, 1, D_MODEL), lambda i, j: (layer, 0, 0)),
            pl.BlockSpec((None, D_MODEL, MAIN_COLB),
                         lambda i, j: (layer, 0, jnp.minimum(j, _GR - 1))),
            gate_half(0), gate_half(1),
            tab_spec, tab_spec,
        ],
        out_specs=[
            pl.BlockSpec((TILE, MAIN_COLB), lambda i, j: (i, j)),
            pl.BlockSpec((TILE, D_MODEL), lambda i, j: (i, 0)),
        ],
        out_shape=[
            jax.ShapeDtypeStruct((t, MAIN_W), _BF16),
            jax.ShapeDtypeStruct((t, D_MODEL), _BF16),
        ],
        compiler_params=pltpu.CompilerParams(
            dimension_semantics=("arbitrary", "arbitrary"),
            vmem_limit_bytes=VMEM_LIMIT),
        name="proj_main",
    )(x, x, x, x, g, w_in, w_in, w_in, cos_r, sin_r)


def _proj_att_kernel(h_ref, wq_ref, wk_ref, wv_ref, ca_ref, sa_ref, o_ref,
                     hs_ref, hq_ref, hp_ref):
    g = pl.program_id(1)
    n_slab = D_MODEL // LANES
    n_blk = TILE // ATT_BLK

    def build_slabs():
        for c in range(TILE // 256):
            y = h_ref[c * 256:(c + 1) * 256, :].astype(_F32)
            for k in range(n_slab):
                hs_ref[k, c * 256:(c + 1) * 256, :] = y[:, k * LANES:(k + 1) * LANES]

    def permute4():
        for b in range(n_blk):
            start = (b // 4) * (4 * ATT_BLK) + (b % 4)
            for k in range(n_slab):
                hp_ref[b * ATT_BLK:(b + 1) * ATT_BLK, k * LANES:(k + 1) * LANES] = (
                    hs_ref[k, pl.ds(start, ATT_BLK, stride=4), :].astype(_BF16))

    def permute16():
        quarter = TILE // 4
        for p0 in range(4):
            for k in range(n_slab):
                hq_ref[k] = hs_ref[k, pl.ds(p0, quarter, stride=4), :]
            for p1 in range(4):
                lo = (p0 * 4 + p1) * ATT_BLK
                for k in range(n_slab):
                    hp_ref[lo:lo + ATT_BLK, k * LANES:(k + 1) * LANES] = (
                        hq_ref[k, pl.ds(p1, ATT_BLK, stride=4), :].astype(_BF16))

    def table_rows(t_ref, c, bi, r):
        if r == 1:
            return t_ref[c * SUB + bi * ATT_BLK:c * SUB + (bi + 1) * ATT_BLK, :]
        if r == 4:
            return t_ref[pl.ds(c * SUB + bi, ATT_BLK, stride=4), :]
        return t_ref[pl.ds(_res16(c, bi), ATT_BLK, stride=16), :]

    lane = lax.broadcasted_iota(jnp.int32, (ATT_BLK, LANES), 1)

    def run(lhs_ref, r):
        for c in range(TILE // SUB):
            lhs = lhs_ref[c * SUB:(c + 1) * SUB, :]
            for col0, w_ref in ((ATT_Q0, wq_ref), (ATT_K0, wk_ref)):
                acc = _dot(lhs, w_ref[...])
                for bi in range(SUB // ATT_BLK):
                    lo = c * SUB + bi * ATT_BLK
                    cos = table_rows(ca_ref, c, bi, r)
                    sin = table_rows(sa_ref, c, bi, r)
                    outs = []
                    for hh in range(ATT_HPG):
                        x = acc[bi * ATT_BLK:(bi + 1) * ATT_BLK, hh * ATT_DH:(hh + 1) * ATT_DH]
                        partner = jnp.where(lane < ATT_ROT_HALF,
                                            pltpu.roll(x, LANES - ATT_ROT_HALF, 1),
                                            pltpu.roll(x, ATT_ROT_HALF, 1))
                        outs.append(x * cos + partner * sin)
                    o_ref[lo:lo + ATT_BLK, col0:col0 + ATT_GW] = (
                        jnp.concatenate(outs, axis=-1).astype(_BF16))
            o_ref[c * SUB:(c + 1) * SUB, ATT_V0:ATT_V0 + ATT_GW] = (
                _dot(lhs, wv_ref[...]).astype(_BF16))

    @pl.when(g == 0)
    def _():
        build_slabs()
        run(h_ref, 1)

    @pl.when(g == 1)
    def _():
        permute4()
        run(hp_ref, 4)

    @pl.when(g == 2)
    def _():
        permute16()
        run(hp_ref, 16)


def _proj_att(h, w_in, layer, cos_a, sin_a):
    t = h.shape[0]
    n_g = len(ATT_GROUPS)
    tiles_per_seq = cos_a.shape[0] // TILE
    tab_spec = pl.BlockSpec((TILE, LANES), lambda i, g: (i % tiles_per_seq, 0))
    att_blk = ATT_COL_LO // ATT_GW

    def w_spec(which):
        return pl.BlockSpec((None, D_MODEL, ATT_GW),
                            lambda i, g: (layer, 0, att_blk + which * n_g + g))

    return pl.pallas_call(
        _proj_att_kernel,
        grid=(t // TILE, n_g),
        in_specs=[
            pl.BlockSpec((TILE, D_MODEL), lambda i, g: (
                jnp.minimum(i + (g >= 1).astype(jnp.int32), t // TILE - 1), 0)),
            w_spec(0), w_spec(1), w_spec(2),
            tab_spec, tab_spec,
        ],
        out_specs=pl.BlockSpec((None, TILE, 3 * ATT_GW), lambda i, g: (g, i, 0)),
        out_shape=jax.ShapeDtypeStruct((n_g, t, 3 * ATT_GW), _BF16),
        scratch_shapes=[
            pltpu.VMEM((D_MODEL // LANES, TILE, LANES), _F32),
            pltpu.VMEM((D_MODEL // LANES, TILE // 4, LANES), _F32),
            pltpu.VMEM((TILE, D_MODEL), _BF16),
        ],
        compiler_params=pltpu.CompilerParams(
            dimension_semantics=("arbitrary", "arbitrary"),
            vmem_limit_bytes=VMEM_LIMIT),
        name="proj_att",
    )(h, w_in, w_in, w_in, cos_a, sin_a)


def _dilated_kernel(cur0, prev0, cur1, prev1, cur2, prev2, o_ref, out_ref, lse_ref):
    i = pl.program_id(1)
    ss = pl.program_id(2)
    n_sub = TILE // SUB
    scale = 1.0 / math.sqrt(ATT_DH)
    ci = lax.broadcasted_iota(jnp.int32, (ATT_BLK, ATT_BLK), 0)
    mi = lax.broadcasted_iota(jnp.int32, (ATT_BLK, ATT_BLK), 1)
    prev_band = mi >= ci
    cur_band = mi <= ci

    groups = ((cur0, prev0, 1), (cur1, prev1, 4), (cur2, prev2, 16))

    def item(g, bi):
        cur_ref, prev_ref, r = groups[g]
        rows = slice(bi * ATT_BLK, (bi + 1) * ATT_BLK)
        if r == 1:
            if bi == 0:
                p_ref, prows = prev_ref, slice(0, ATT_BLK)
                has_prev = (i > 0) | (ss > 0)
            else:
                p_ref, prows = cur_ref, slice((bi - 1) * ATT_BLK, bi * ATT_BLK)
                has_prev = None
            dst = pl.ds(pl.multiple_of(ss * SUB + bi * ATT_BLK, ATT_BLK), ATT_BLK)
        elif r == 4:
            p_ref, prows = prev_ref, rows
            has_prev = (i > 0) | (ss > 0)
            dst = pl.ds(ss * SUB + bi, ATT_BLK, stride=4)
        else:
            p_ref, prows = prev_ref, rows
            has_prev = i > 0
            dst = pl.ds(_res16(ss, bi), ATT_BLK, stride=16)
        pmask = prev_band if has_prev is None else (prev_band & has_prev)
        return dict(g=g, cur_ref=cur_ref, p_ref=p_ref, rows=rows, prows=prows,
                    pmask=pmask, dst=dst)

    def head_cols(col0, h):
        return slice(col0 + h * ATT_DH, col0 + (h + 1) * ATT_DH)

    def score_stage(it):
        out = []
        for h in range(ATT_HPG):
            q = it["cur_ref"][it["rows"], head_cols(ATT_Q0, h)]
            out.append((_dot_nt(q, it["p_ref"][it["prows"], head_cols(ATT_K0, h)]),
                        _dot_nt(q, it["cur_ref"][it["rows"], head_cols(ATT_K0, h)])))
        return out

    def softmax_stage(it, scores):
        out = []
        for sp, sc in scores:
            sp = jnp.where(it["pmask"], sp * scale, NEG)
            sc = jnp.where(cur_band, sc * scale, NEG)
            m = jnp.max(jnp.maximum(sp, sc), axis=-1, keepdims=True)
            pp = jnp.exp(sp - m)
            pc = jnp.exp(sc - m)
            l = jnp.sum(pp + pc, axis=-1, keepdims=True)
            out.append((pp.astype(_BF16), pc.astype(_BF16), m, l))
        return out

    def value_stage(it, probs):
        for h, (pp, pc, m, l) in enumerate(probs):
            num = (_dot(pp, it["p_ref"][it["prows"], head_cols(ATT_V0, h)])
                   + _dot(pc, it["cur_ref"][it["rows"], head_cols(ATT_V0, h)]))
            out_ref[it["g"], h, it["dst"], :] = num * (1.0 / l)
            lse_ref[it["g"], h, it["dst"], :] = jnp.broadcast_to(m + jnp.log(l),
                                                                 (ATT_BLK, LANES))

    items = [item(g, bi) for g in range(len(groups)) for bi in range(SUB // ATT_BLK)]
    scores = score_stage(items[0])
    for n, it in enumerate(items):
        nxt = score_stage(items[n + 1]) if n + 1 < len(items) else None
        value_stage(it, softmax_stage(it, scores))
        scores = nxt

    @pl.when(ss == n_sub - 1)
    def _():
        n_g = len(ATT_GROUPS)

        def body(c, carry):
            rows = pl.ds(pl.multiple_of(c * ATT_BLK, ATT_BLK), ATT_BLK)
            outs = []
            for h in range(ATT_HPG):
                lses = [lse_ref[g, h, rows, :] for g in range(n_g)]
                top = functools.reduce(jnp.maximum, lses)
                ws = [jnp.exp(x - top) for x in lses]
                inv = 1.0 / sum(ws)
                outs.append(sum(ws[g] * out_ref[g, h, rows, :] for g in range(n_g)) * inv)
            o_ref[rows, :] = jnp.concatenate(outs, axis=-1).astype(_BF16)
            return carry
        lax.fori_loop(0, TILE // ATT_BLK, body, 0)


def _dilated(att, batch, seq):
    tiles = seq // TILE
    n_sub = TILE // SUB
    subs_per_seq = seq // SUB
    per = SUB // ATT_BLK

    def cur(g):
        return pl.BlockSpec(
            (None, SUB, 3 * ATT_GW),
            lambda b, i, s: (g, b * subs_per_seq + i * n_sub + s, 0))

    def prev_sub(g, back):
        return pl.BlockSpec(
            (None, SUB, 2 * ATT_GW),
            lambda b, i, s: (g, b * subs_per_seq + jnp.maximum(i * n_sub + s - back, 0), 0))

    def prev_blk(g):
        return pl.BlockSpec(
            (None, ATT_BLK, 2 * ATT_GW),
            lambda b, i, s: (g, b * subs_per_seq * per
                             + jnp.maximum((i * n_sub + s) * per - 1, 0), 0))

    in_specs = [cur(0), prev_blk(0), cur(1), prev_sub(1, 1), cur(2), prev_sub(2, n_sub)]

    return pl.pallas_call(
        _dilated_kernel,
        grid=(batch, tiles, n_sub),
        in_specs=in_specs,
        out_specs=pl.BlockSpec((TILE, ATT_GW), lambda b, i, s: (b * tiles + i, 0)),
        out_shape=jax.ShapeDtypeStruct((batch * seq, ATT_GW), _BF16),
        scratch_shapes=[
            pltpu.VMEM((len(ATT_GROUPS), ATT_HPG, TILE, LANES), _F32),
            pltpu.VMEM((len(ATT_GROUPS), ATT_HPG, TILE, LANES), _F32),
        ],
        compiler_params=pltpu.CompilerParams(
            dimension_semantics=("arbitrary", "arbitrary", "arbitrary"),
            vmem_limit_bytes=VMEM_LIMIT),
        name="dilated",
    )(*([att] * 6))


def _ret_merge_kernel(q_ref, k_ref, v_ref, g_ref, dm_ref, xi_ref, ze_ref, dec_ref,
                      x_ref, ya_ref, gr_ref, ga_ref, wr_ref, wa_ref, wo_ref, gn_ref,
                      o_ref, r_ref, yr_prev, yr_cur, mg_ref, m_ref, *, n_chunks, blocks_per_seq):
    s = pl.program_id(0)

    @pl.when(s % blocks_per_seq == 0)
    def _():
        r_ref[...] = jnp.zeros_like(r_ref)

    @pl.when(s == 0)
    def _():
        yr_prev[...] = jnp.zeros_like(yr_prev)

    pw = D_MODEL // MERGE_PIECES

    def branch_piece(p):
        cols = slice(p * pw, (p + 1) * pw)
        y_ret = _dot(yr_prev[...], wr_ref[:, cols])
        y_att = _dot(ya_ref[...], wa_ref[:, cols])
        mg_ref[:, cols] = (jax.nn.sigmoid(gr_ref[:, cols].astype(_F32)) * y_ret
                           + jax.nn.sigmoid(ga_ref[:, cols].astype(_F32)) * y_att
                           ).astype(_BF16)

    def out_piece(p):
        cols = slice(p * pw, (p + 1) * pw)
        m_ref[:, cols] = _dot(mg_ref[...], wo_ref[:, cols])

    def retention_unit(c, h):
        rows = slice(c * RET_C, (c + 1) * RET_C)
        q = q_ref[rows, h * RET_DK:(h + 1) * RET_DK]
        k = k_ref[rows, h * RET_DK:(h + 1) * RET_DK]
        v = v_ref[rows, h * RET_DV:(h + 1) * RET_DV]
        scores = _dot_nt(q, k) * dm_ref[h]
        qx = (q.astype(_F32) * xi_ref[h]).astype(_BF16)
        r_old = r_ref[h]
        y = _dot(jnp.concatenate([scores.astype(_BF16), qx], axis=1),
                 jnp.concatenate([v, r_old.astype(_BF16)], axis=0))
        kz = (k.astype(_F32) * ze_ref[h]).astype(_BF16)
        r_ref[h] = dec_ref[h] * r_old + _dot_tn(kz, v)
        mu = jnp.mean(y, axis=-1, keepdims=True)
        yc = y - mu
        var = jnp.mean(yc * yc, axis=-1, keepdims=True)
        yn = yc * lax.rsqrt(var + NORM_EPS)
        gate = g_ref[rows, h * RET_DV:(h + 1) * RET_DV].astype(_F32)
        yr_cur[rows, h * RET_DV:(h + 1) * RET_DV] = (
            gate * jax.nn.sigmoid(gate) * yn).astype(_BF16)

    units = [(c, h) for c in range(n_chunks) for h in range(RET_HEADS)]
    assert len(units) == 2 * MERGE_PIECES == 8
    order = "R B R R B R B R R B R O O R O O".split()
    nxt = {"R": iter(units), "B": iter(range(MERGE_PIECES)), "O": iter(range(MERGE_PIECES))}
    for kind in order:
        if kind == "R":
            retention_unit(*next(nxt["R"]))
        elif kind == "B":
            branch_piece(next(nxt["B"]))
        else:
            out_piece(next(nxt["O"]))
    o_ref[...] = x_ref[...] + _rms(m_ref[...], gn_ref[...])
    yr_prev[...] = yr_cur[...]


def _resident(stacked, layer):
    shape = stacked.shape[1:]
    return pl.BlockSpec((None,) + shape, lambda i: (layer,) + (0,) * len(shape),
                        pipeline_mode=pl.Buffered(1))


def _ret_merge(x, proj, ya, consts, wr, wa, wo, g, layer, seq, tc=512):
    t = x.shape[0]
    n_blocks = t // tc
    dm, xi, ze, dec = consts
    qk_w = RET_HEADS * RET_DK
    v_w = RET_HEADS * RET_DV
    colb = MAIN_COLB

    def ret_blk(width, col):
        return pl.BlockSpec((tc, width), lambda s: (jnp.minimum(s, n_blocks - 1), col))

    def merge_blk(width, col):
        return pl.BlockSpec((tc, width), lambda s: (jnp.maximum(s - 1, 0), col))

    def const_spec(a):
        return pl.BlockSpec(a.shape, lambda s: (0,) * a.ndim)

    return pl.pallas_call(
        functools.partial(_ret_merge_kernel, n_chunks=tc // RET_C,
                          blocks_per_seq=seq // tc),
        grid=(n_blocks + 1,),
        in_specs=[
            ret_blk(qk_w, _RQ * colb // qk_w), ret_blk(qk_w, _RK * colb // qk_w),
            ret_blk(v_w, _RV * colb // v_w), ret_blk(v_w, _RG * colb // v_w),
            const_spec(dm), const_spec(xi), const_spec(ze), const_spec(dec),
            merge_blk(D_MODEL, 0), merge_blk(ya.shape[1], 0),
            merge_blk(D_MODEL, _GR * colb // D_MODEL), merge_blk(D_MODEL, _GA * colb // D_MODEL),
            _resident(wr, layer), _resident(wa, layer), _resident(wo, layer),
            _resident(g, layer),
        ],
        out_specs=merge_blk(D_MODEL, 0),
        out_shape=jax.ShapeDtypeStruct((t, D_MODEL), _F32),
        scratch_shapes=[
            pltpu.VMEM((RET_HEADS, RET_DK, RET_DV), _F32),
            pltpu.VMEM((tc, v_w), _BF16),
            pltpu.VMEM((tc, v_w), _BF16),
            pltpu.VMEM((tc, D_MODEL), _BF16),
            pltpu.VMEM((tc, D_MODEL), _F32),
        ],
        compiler_params=pltpu.CompilerParams(
            dimension_semantics=("arbitrary",), vmem_limit_bytes=VMEM_LIMIT),
        name="ret_merge",
    )(proj, proj, proj, proj, dm, xi, ze, dec, x, ya, proj, proj, wr, wa, wo, g)


def _ffn_kernel(x_ref, gpre_ref, wg_ref, wu_ref, wd_ref, gpost_ref, o_ref, a_ref, *, bounds):
    for r0 in range(0, x_ref.shape[0], FFN_ROWS):
        rows = slice(r0, r0 + FFN_ROWS)
        x = x_ref[rows, :]
        h = _rms(x, gpre_ref[...]).astype(_BF16)
        for lo, hi in zip(bounds[:-1], bounds[1:]):
            cols = slice(lo, hi)
            gate = _dot(h, wg_ref[:, cols])
            up = _dot(h, wu_ref[:, cols])
            a_ref[rows, cols] = (gate * jax.nn.sigmoid(gate) * up).astype(_BF16)
        f = _dot(a_ref[rows, :], wd_ref[...])
        o_ref[rows, :] = x + _rms(f, gpost_ref[...])


def _ffn(x, gpre, wg, wu, wd, gpost, layer, tm=1024):
    t = x.shape[0]
    bounds = (0, 3 * MXU_W, 6 * MXU_W, 9 * MXU_W, D_FF)
    return pl.pallas_call(
        functools.partial(_ffn_kernel, bounds=bounds),
        grid=(t // tm,),
        in_specs=[
            pl.BlockSpec((tm, D_MODEL), lambda i: (i, 0)),
            _resident(gpre, layer), _resident(wg, layer), _resident(wu, layer),
            _resident(wd, layer), _resident(gpost, layer),
        ],
        out_specs=pl.BlockSpec((tm, D_MODEL), lambda i: (i, 0)),
        out_shape=jax.ShapeDtypeStruct((t, D_MODEL), _F32),
        scratch_shapes=[pltpu.VMEM((tm, D_FF), _BF16)],
        compiler_params=pltpu.CompilerParams(
            dimension_semantics=("arbitrary",), vmem_limit_bytes=VMEM_LIMIT),
        name="ffn",
    )(x, gpre, wg, wu, wd, gpost)


def _cos_sin(seq, inv_freq, coarse=64):
    hi = (jnp.arange(seq // coarse, dtype=_F32) * coarse)[:, None] * inv_freq[None, :]
    lo = jnp.arange(coarse, dtype=_F32)[:, None] * inv_freq[None, :]
    ch, sh = jnp.cos(hi)[:, None, :], jnp.sin(hi)[:, None, :]
    cl, sl = jnp.cos(lo)[None, :, :], jnp.sin(lo)[None, :, :]
    n = inv_freq.shape[0]
    return (ch * cl - sh * sl).reshape(seq, n), (sh * cl + ch * sl).reshape(seq, n)


def _rotary_tables(seq):
    half = RET_DK // 2
    cos_r, sin_r = _cos_sin(seq, RET_THETA ** (-jnp.arange(half, dtype=_F32) / half))
    cos, sin = _cos_sin(
        seq, ATT_THETA ** (-jnp.arange(ATT_ROT_HALF, dtype=_F32) / ATT_ROT_HALF))
    rest = LANES - 2 * ATT_ROT_HALF
    cos_a = jnp.concatenate([cos, cos, jnp.ones((seq, rest), _F32)], axis=-1)
    sin_a = jnp.concatenate([-sin, sin, jnp.zeros((seq, rest), _F32)], axis=-1)
    return cos_r, sin_r, cos_a, sin_a


def _retention_consts():
    c = RET_C
    lg = jnp.log(1.0 - 2.0 ** (-5.0 - jnp.arange(RET_HEADS, dtype=_F32)))
    pos = jnp.arange(c, dtype=_F32)
    rel = pos[:, None] - pos[None, :]
    dmask = jnp.where(rel[None] >= 0,
                      jnp.exp(jnp.maximum(rel, 0.0)[None] * lg[:, None, None]), 0.0)
    xi = jnp.exp((pos + 1.0)[None, :] * lg[:, None])[:, :, None]
    zeta = jnp.exp((c - 1.0 - pos)[None, :] * lg[:, None])[:, :, None]
    decay = jnp.broadcast_to(jnp.exp(c * lg)[:, None, None], (RET_HEADS, 1, RET_DV))
    return dmask, xi, zeta, decay


def kernel(x, w_in, w_ret_out, w_att_out, w_o, w_ffn_gate, w_ffn_up, w_ffn_down,
           g_pre_mix, g_post_mix, g_pre_ffn, g_post_ffn):
    batch, seq, d = x.shape
    depth = w_in.shape[0]
    assert d == D_MODEL and seq % TILE == 0
    cos_r, sin_r, cos_a, sin_a = _rotary_tables(seq)
    consts = _retention_consts()
    xf = x.reshape(batch * seq, d)
    w_in, w_ret_out, w_att_out, w_o, w_ffn_gate, w_ffn_up, w_ffn_down = (
        w.astype(_BF16) for w in (w_in, w_ret_out, w_att_out, w_o,
                                  w_ffn_gate, w_ffn_up, w_ffn_down))
    g_pre_mix, g_post_mix, g_pre_ffn, g_post_ffn = (
        g[:, None, :] for g in (g_pre_mix, g_post_mix, g_pre_ffn, g_post_ffn))
    for l in range(depth):
        proj, h = _proj_main(xf, g_pre_mix, w_in, l, cos_r, sin_r)
        att = _proj_att(h, w_in, l, cos_a, sin_a)
        ya = _dilated(att, batch, seq)
        xf = _ret_merge(xf, proj, ya, consts, w_ret_out, w_att_out, w_o, g_post_mix, l, seq)
        xf = _ffn(xf, g_pre_ffn, w_ffn_gate, w_ffn_up, w_ffn_down, g_post_ffn, l)
    return xf.reshape(batch, seq, d)
```

```python
import functools
import math

import jax
import jax.numpy as jnp
import numpy as np
from jax import lax
from jax.experimental import pallas as pl
from jax.experimental.pallas import tpu as pltpu

D_MODEL = 1024
RET_HEADS = 4
RET_DK = 256
RET_DV = 512
RET_THETA = 10000.0
ATT_GROUPS = ((128, 1), (512, 4), (2048, 16))
ATT_HPG = 4
ATT_DH = 128
ATT_ROT_HALF = 16
ATT_THETA = 500000.0
D_FF = 2816
NORM_EPS = 1e-6
NEG = -1e30

LANES = 128
MXU_W = 256
ATT_BLK = 128
TILE = 2048
SUB = 512
RET_C = 256
MERGE_PIECES = 4
FFN_ROWS = 512
VMEM_LIMIT = 60 * 1024 * 1024

MAIN_W = 8192
MAIN_COLB = 1024
_RQ, _RK, _RV, _RG, _GR, _GA = 0, 1, 2, 4, 6, 7
ATT_GW = ATT_HPG * ATT_DH
ATT_K0, ATT_V0, ATT_Q0 = 0, ATT_GW, 2 * ATT_GW
ATT_COL_LO = 2 * RET_HEADS * RET_DK + 2 * RET_HEADS * RET_DV
GATE_COL_LO = ATT_COL_LO + 3 * len(ATT_GROUPS) * ATT_GW

_F32 = jnp.float32
_BF16 = jnp.bfloat16


def _dot(a, b):
    return jnp.dot(a, b, preferred_element_type=_F32)


def _dot_nt(a, b):
    return lax.dot_general(a, b, (((1,), (1,)), ((), ())), preferred_element_type=_F32)


def _dot_tn(a, b):
    return lax.dot_general(a, b, (((0,), (0,)), ((), ())), preferred_element_type=_F32)


def _res16(sub, blk):
    return 4 * blk + sub


def _rms(x, g):
    return x * lax.rsqrt(jnp.mean(x * x, axis=-1, keepdims=True) + NORM_EPS) * g


def _proj_main_kernel(x0_ref, x1_ref, x2_ref, x3_ref, g_ref, w_ref, wlo_ref, whi_ref,
                      cr_ref, sr_ref, o_ref, h_ref):
    j = pl.program_id(1)
    x_refs = (x0_ref, x1_ref, x2_ref, x3_ref)

    def norm_chunk(c):
        for r in range(0, SUB, 256):
            h_ref[c * SUB + r:c * SUB + r + 256, :] = _rms(
                x_refs[c][r:r + 256, :], g_ref[...]).astype(_BF16)

    def run(epilogue, prologue=None):
        for c in range(TILE // SUB):
            if prologue is not None:
                prologue(c)
            epilogue(_dot(h_ref[c * SUB:(c + 1) * SUB, :], w_ref[...]), c)

    def plain(acc, c):
        o_ref[c * SUB:(c + 1) * SUB, :] = acc.astype(_BF16)

    def run_halves():
        half = MAIN_COLB // 2
        for c in range(TILE // SUB):
            lhs = h_ref[c * SUB:(c + 1) * SUB, :]
            o_ref[c * SUB:(c + 1) * SUB, :half] = _dot(lhs, wlo_ref[...]).astype(_BF16)
            o_ref[c * SUB:(c + 1) * SUB, half:] = _dot(lhs, whi_ref[...]).astype(_BF16)

    def ret_rotary(scale):
        def f(acc, c):
            for bi in range(SUB // ATT_BLK):
                lo = c * SUB + bi * ATT_BLK
                cos = cr_ref[lo:lo + ATT_BLK, :]
                sin = sr_ref[lo:lo + ATT_BLK, :]
                a = acc[bi * ATT_BLK:(bi + 1) * ATT_BLK, :]
                outs = []
                for hh in range(MAIN_COLB // RET_DK):
                    x1 = a[:, hh * RET_DK:hh * RET_DK + LANES]
                    x2 = a[:, hh * RET_DK + LANES:(hh + 1) * RET_DK]
                    outs.append((x1 * cos - x2 * sin) * scale)
                    outs.append((x1 * sin + x2 * cos) * scale)
                o_ref[lo:lo + ATT_BLK, :] = jnp.concatenate(outs, axis=-1).astype(_BF16)
        return f

    @pl.when(j == _RQ)
    def _():
        run(ret_rotary(1.0), prologue=norm_chunk)

    @pl.when(j == _RK)
    def _():
        run(ret_rotary(RET_DK ** -0.5))

    @pl.when((j >= _RV) & (j < _GR))
    def _():
        run(plain)

    @pl.when(j >= _GR)
    def _():
        run_halves()


def _proj_main(x, g, w_in, layer, cos_r, sin_r):
    t = x.shape[0]
    n_tiles = t // TILE
    n_col = MAIN_W // MAIN_COLB
    tiles_per_seq = cos_r.shape[0] // TILE
    half = MAIN_COLB // 2
    gate_blk = GATE_COL_LO // half

    def tile_from(i, j, switch_at):
        return jnp.minimum(i + (j >= switch_at).astype(jnp.int32), n_tiles - 1)

    def x_chunk(c):
        per = TILE // SUB
        return pl.BlockSpec(
            (SUB, D_MODEL), lambda i, j: (tile_from(i, j, n_col - per + c) * per + c, 0))

    tab_spec = pl.BlockSpec(
        (TILE, LANES), lambda i, j: (tile_from(i, j, _RV) % tiles_per_seq, 0))

    def gate_half(which):
        return pl.BlockSpec(
            (None, D_MODEL, half),
            lambda i, j: (layer, 0, gate_blk + 2 * jnp.maximum(j - _GR, 0) + which))

    return pl.pallas_call(
        _proj_main_kernel,
        grid=(n_tiles, n_col),
        in_specs=[
            x_chunk(0), x_chunk(1), x_chunk(2), x_chunk(3),
            pl.BlockSpec((None, 1, D_MODEL), lambda i, j: (layer, 0, 0)),
            pl.BlockSpec((None, D_MODEL, MAIN_COLB),
                         lambda i, j: (layer, 0, jnp.minimum(j, _GR - 1))),
            gate_half(0), gate_half(1),
            tab_spec, tab_spec,
        ],
        out_specs=[
            pl.BlockSpec((TILE, MAIN_COLB), lambda i, j: (i, j)),
            pl.BlockSpec((TILE, D_MODEL), lambda i, j: (i, 0)),
        ],
        out_shape=[
            jax.ShapeDtypeStruct((t, MAIN_W), _BF16),
            jax.ShapeDtypeStruct((t, D_MODEL), _BF16),
        ],
        compiler_params=pltpu.CompilerParams(
            dimension_semantics=("arbitrary", "arbitrary"),
            vmem_limit_bytes=VMEM_LIMIT),
        name="proj_main",
    )(x, x, x, x, g, w_in, w_in, w_in, cos_r, sin_r)


def _proj_att_kernel(h_ref, wq_ref, wk_ref, wv_ref, ca_ref, sa_ref, o_ref,
                     hs_ref, hq_ref, hp_ref):
    g = pl.program_id(1)
    n_slab = D_MODEL // LANES
    n_blk = TILE // ATT_BLK

    def build_slabs():
        for c in range(TILE // 256):
            y = h_ref[c * 256:(c + 1) * 256, :].astype(_F32)
            for k in range(n_slab):
                hs_ref[k, c * 256:(c + 1) * 256, :] = y[:, k * LANES:(k + 1) * LANES]

    def permute4():
        for b in range(n_blk):
            start = (b // 4) * (4 * ATT_BLK) + (b % 4)
            for k in range(n_slab):
                hp_ref[b * ATT_BLK:(b + 1) * ATT_BLK, k * LANES:(k + 1) * LANES] = (
                    hs_ref[k, pl.ds(start, ATT_BLK, stride=4), :].astype(_BF16))

    def permute16():
        quarter = TILE // 4
        for p0 in range(4):
            for k in range(n_slab):
                hq_ref[k] = hs_ref[k, pl.ds(p0, quarter, stride=4), :]
            for p1 in range(4):
                lo = (p0 * 4 + p1) * ATT_BLK
                for k in range(n_slab):
                    hp_ref[lo:lo + ATT_BLK, k * LANES:(k + 1) * LANES] = (
                        hq_ref[k, pl.ds(p1, ATT_BLK, stride=4), :].astype(_BF16))

    def table_rows(t_ref, c, bi, r):
        if r == 1:
            return t_ref[c * SUB + bi * ATT_BLK:c * SUB + (bi + 1) * ATT_BLK, :]
        if r == 4:
            return t_ref[pl.ds(c * SUB + bi, ATT_BLK, stride=4), :]
        return t_ref[pl.ds(_res16(c, bi), ATT_BLK, stride=16), :]

    lane = lax.broadcasted_iota(jnp.int32, (ATT_BLK, LANES), 1)

    def run(lhs_ref, r):
        for c in range(TILE // SUB):
            lhs = lhs_ref[c * SUB:(c + 1) * SUB, :]
            for col0, w_ref in ((ATT_Q0, wq_ref), (ATT_K0, wk_ref)):
                acc = _dot(lhs, w_ref[...])
                for bi in range(SUB // ATT_BLK):
                    lo = c * SUB + bi * ATT_BLK
                    cos = table_rows(ca_ref, c, bi, r)
                    sin = table_rows(sa_ref, c, bi, r)
                    outs = []
                    for hh in range(ATT_HPG):
                        x = acc[bi * ATT_BLK:(bi + 1) * ATT_BLK, hh * ATT_DH:(hh + 1) * ATT_DH]
                        partner = jnp.where(lane < ATT_ROT_HALF,
                                            pltpu.roll(x, LANES - ATT_ROT_HALF, 1),
                                            pltpu.roll(x, ATT_ROT_HALF, 1))
                        outs.append(x * cos + partner * sin)
                    o_ref[lo:lo + ATT_BLK, col0:col0 + ATT_GW] = (
                        jnp.concatenate(outs, axis=-1).astype(_BF16))
            o_ref[c * SUB:(c + 1) * SUB, ATT_V0:ATT_V0 + ATT_GW] = (
                _dot(lhs, wv_ref[...]).astype(_BF16))

    @pl.when(g == 0)
    def _():
        build_slabs()
        run(h_ref, 1)

    @pl.when(g == 1)
    def _():
        permute4()
        run(hp_ref, 4)

    @pl.when(g == 2)
    def _():
        permute16()
        run(hp_ref, 16)


def _proj_att(h, w_in, layer, cos_a, sin_a):
    t = h.shape[0]
    n_g = len(ATT_GROUPS)
    tiles_per_seq = cos_a.shape[0] // TILE
    tab_spec = pl.BlockSpec((TILE, LANES), lambda i, g: (i % tiles_per_seq, 0))
    att_blk = ATT_COL_LO // ATT_GW

    def w_spec(which):
        return pl.BlockSpec((None, D_MODEL, ATT_GW),
                            lambda i, g: (layer, 0, att_blk + which * n_g + g))

    return pl.pallas_call(
        _proj_att_kernel,
        grid=(t // TILE, n_g),
        in_specs=[
            pl.BlockSpec((TILE, D_MODEL), lambda i, g: (
                jnp.minimum(i + (g >= 1).astype(jnp.int32), t // TILE - 1), 0)),
            w_spec(0), w_spec(1), w_spec(2),
            tab_spec, tab_spec,
        ],
        out_specs=pl.BlockSpec((None, TILE, 3 * ATT_GW), lambda i, g: (g, i, 0)),
        out_shape=jax.ShapeDtypeStruct((n_g, t, 3 * ATT_GW), _BF16),
        scratch_shapes=[
            pltpu.VMEM((D_MODEL // LANES, TILE, LANES), _F32),
            pltpu.VMEM((D_MODEL // LANES, TILE // 4, LANES), _F32),
            pltpu.VMEM((TILE, D_MODEL), _BF16),
        ],
        compiler_params=pltpu.CompilerParams(
            dimension_semantics=("arbitrary", "arbitrary"),
            vmem_limit_bytes=VMEM_LIMIT),
        name="proj_att",
    )(h, w_in, w_in, w_in, cos_a, sin_a)


def _dilated_kernel(cur0, prev0, cur1, prev1, cur2, prev2, o_ref, out_ref, lse_ref):
    i = pl.program_id(1)
    ss = pl.program_id(2)
    n_sub = TILE // SUB
    scale = 1.0 / math.sqrt(ATT_DH)
    ci = lax.broadcasted_iota(jnp.int32, (ATT_BLK, ATT_BLK), 0)
    mi = lax.broadcasted_iota(jnp.int32, (ATT_BLK, ATT_BLK), 1)
    prev_band = mi >= ci
    cur_band = mi <= ci

    groups = ((cur0, prev0, 1), (cur1, prev1, 4), (cur2, prev2, 16))

    def item(g, bi):
        cur_ref, prev_ref, r = groups[g]
        rows = slice(bi * ATT_BLK, (bi + 1) * ATT_BLK)
        if r == 1:
            if bi == 0:
                p_ref, prows = prev_ref, slice(0, ATT_BLK)
                has_prev = (i > 0) | (ss > 0)
            else:
                p_ref, prows = cur_ref, slice((bi - 1) * ATT_BLK, bi * ATT_BLK)
                has_prev = None
            dst = pl.ds(pl.multiple_of(ss * SUB + bi * ATT_BLK, ATT_BLK), ATT_BLK)
        elif r == 4:
            p_ref, prows = prev_ref, rows
            has_prev = (i > 0) | (ss > 0)
            dst = pl.ds(ss * SUB + bi, ATT_BLK, stride=4)
        else:
            p_ref, prows = prev_ref, rows
            has_prev = i > 0
            dst = pl.ds(_res16(ss, bi), ATT_BLK, stride=16)
        pmask = prev_band if has_prev is None else (prev_band & has_prev)
        return dict(g=g, cur_ref=cur_ref, p_ref=p_ref, rows=rows, prows=prows,
                    pmask=pmask, dst=dst)

    def head_cols(col0, h):
        return slice(col0 + h * ATT_DH, col0 + (h + 1) * ATT_DH)

    def score_stage(it):
        out = []
        for h in range(ATT_HPG):
            q = it["cur_ref"][it["rows"], head_cols(ATT_Q0, h)]
            out.append((_dot_nt(q, it["p_ref"][it["prows"], head_cols(ATT_K0, h)]),
                        _dot_nt(q, it["cur_ref"][it["rows"], head_cols(ATT_K0, h)])))
        return out

    def softmax_stage(it, scores):
        out = []
        for sp, sc in scores:
            sp = jnp.where(it["pmask"], sp * scale, NEG)
            sc = jnp.where(cur_band, sc * scale, NEG)
            m = jnp.max(jnp.maximum(sp, sc), axis=-1, keepdims=True)
            pp = jnp.exp(sp - m)
            pc = jnp.exp(sc - m)
            l = jnp.sum(pp + pc, axis=-1, keepdims=True)
            out.append((pp.astype(_BF16), pc.astype(_BF16), m, l))
        return out

    def value_stage(it, probs):
        for h, (pp, pc, m, l) in enumerate(probs):
            num = (_dot(pp, it["p_ref"][it["prows"], head_cols(ATT_V0, h)])
                   + _dot(pc, it["cur_ref"][it["rows"], head_cols(ATT_V0, h)]))
            out_ref[it["g"], h, it["dst"], :] = num * (1.0 / l)
            lse_ref[it["g"], h, it["dst"], :] = jnp.broadcast_to(m + jnp.log(l),
                                                                 (ATT_BLK, LANES))

    items = [item(g, bi) for g in range(len(groups)) for bi in range(SUB // ATT_BLK)]
    scores = score_stage(items[0])
    for n, it in enumerate(items):
        nxt = score_stage(items[n + 1]) if n + 1 < len(items) else None
        value_stage(it, softmax_stage(it, scores))
        scores = nxt

    @pl.when(ss == n_sub - 1)
    def _():
        n_g = len(ATT_GROUPS)

        def body(c, carry):
            rows = pl.ds(pl.multiple_of(c * ATT_BLK, ATT_BLK), ATT_BLK)
            outs = []
            for h in range(ATT_HPG):
                lses = [lse_ref[g, h, rows, :] for g in range(n_g)]
                top = functools.reduce(jnp.maximum, lses)
                ws = [jnp.exp(x - top) for x in lses]
                inv = 1.0 / sum(ws)
                outs.append(sum(ws[g] * out_ref[g, h, rows, :] for g in range(n_g)) * inv)
            o_ref[rows, :] = jnp.concatenate(outs, axis=-1).astype(_BF16)
            return carry
        lax.fori_loop(0, TILE // ATT_BLK, body, 0)


def _dilated(att, batch, seq):
    tiles = seq // TILE
    n_sub = TILE // SUB
    subs_per_seq = seq // SUB
    per = SUB // ATT_BLK

    def cur(g):
        return pl.BlockSpec(
            (None, SUB, 3 * ATT_GW),
            lambda b, i, s: (g, b * subs_per_seq + i * n_sub + s, 0))

    def prev_sub(g, back):
        return pl.BlockSpec(
            (None, SUB, 2 * ATT_GW),
            lambda b, i, s: (g, b * subs_per_seq + jnp.maximum(i * n_sub + s - back, 0), 0))

    def prev_blk(g):
        return pl.BlockSpec(
            (None, ATT_BLK, 2 * ATT_GW),
            lambda b, i, s: (g, b * subs_per_seq * per
                             + jnp.maximum((i * n_sub + s) * per - 1, 0), 0))

    in_specs = [cur(0), prev_blk(0), cur(1), prev_sub(1, 1), cur(2), prev_sub(2, n_sub)]

    return pl.pallas_call(
        _dilated_kernel,
        grid=(batch, tiles, n_sub),
        in_specs=in_specs,
        out_specs=pl.BlockSpec((TILE, ATT_GW), lambda b, i, s: (b * tiles + i, 0)),
        out_shape=jax.ShapeDtypeStruct((batch * seq, ATT_GW), _BF16),
        scratch_shapes=[
            pltpu.VMEM((len(ATT_GROUPS), ATT_HPG, TILE, LANES), _F32),
            pltpu.VMEM((len(ATT_GROUPS), ATT_HPG, TILE, LANES), _F32),
        ],
        compiler_params=pltpu.CompilerParams(
            dimension_semantics=("arbitrary", "arbitrary", "arbitrary"),
            vmem_limit_bytes=VMEM_LIMIT),
        name="dilated",
    )(*([att] * 6))


def _ret_merge_kernel(q_ref, k_ref, v_ref, g_ref, dm_ref, xi_ref, ze_ref, dec_ref,
                      x_ref, ya_ref, gr_ref, ga_ref, wr_ref, wa_ref, wo_ref, gn_ref,
                      o_ref, r_ref, yr_prev, yr_cur, mg_ref, m_ref, *, n_chunks, blocks_per_seq):
    s = pl.program_id(0)

    @pl.when(s % blocks_per_seq == 0)
    def _():
        r_ref[...] = jnp.zeros_like(r_ref)

    @pl.when(s == 0)
    def _():
        yr_prev[...] = jnp.zeros_like(yr_prev)

    pw = D_MODEL // MERGE_PIECES

    def branch_piece(p):
        cols = slice(p * pw, (p + 1) * pw)
        y_ret = _dot(yr_prev[...], wr_ref[:, cols].astype(_BF16))
        y_att = _dot(ya_ref[...], wa_ref[:, cols].astype(_BF16))
        mg_ref[:, cols] = (jax.nn.sigmoid(gr_ref[:, cols].astype(_F32)) * y_ret
                           + jax.nn.sigmoid(ga_ref[:, cols].astype(_F32)) * y_att
                           ).astype(_BF16)

    def out_piece(p):
        cols = slice(p * pw, (p + 1) * pw)
        m_ref[:, cols] = _dot(mg_ref[...], wo_ref[:, cols].astype(_BF16))

    def retention_unit(c, h):
        rows = slice(c * RET_C, (c + 1) * RET_C)
        q = q_ref[rows, h * RET_DK:(h + 1) * RET_DK]
        k = k_ref[rows, h * RET_DK:(h + 1) * RET_DK]
        v = v_ref[rows, h * RET_DV:(h + 1) * RET_DV]
        scores = _dot_nt(q, k) * dm_ref[h]
        qx = (q.astype(_F32) * xi_ref[h]).astype(_BF16)
        r_old = r_ref[h]
        y = _dot(jnp.concatenate([scores.astype(_BF16), qx], axis=1),
                 jnp.concatenate([v, r_old.astype(_BF16)], axis=0))
        kz = (k.astype(_F32) * ze_ref[h]).astype(_BF16)
        r_ref[h] = dec_ref[h] * r_old + _dot_tn(kz, v)
        mu = jnp.mean(y, axis=-1, keepdims=True)
        yc = y - mu
        var = jnp.mean(yc * yc, axis=-1, keepdims=True)
        yn = yc * lax.rsqrt(var + NORM_EPS)
        gate = g_ref[rows, h * RET_DV:(h + 1) * RET_DV].astype(_F32)
        yr_cur[rows, h * RET_DV:(h + 1) * RET_DV] = (
            gate * jax.nn.sigmoid(gate) * yn).astype(_BF16)

    units = [(c, h) for c in range(n_chunks) for h in range(RET_HEADS)]
    assert len(units) == 2 * MERGE_PIECES == 8
    order = "R B R R B R B R R B R O O R O O".split()
    nxt = {"R": iter(units), "B": iter(range(MERGE_PIECES)), "O": iter(range(MERGE_PIECES))}
    for kind in order:
        if kind == "R":
            retention_unit(*next(nxt["R"]))
        elif kind == "B":
            branch_piece(next(nxt["B"]))
        else:
            out_piece(next(nxt["O"]))
    o_ref[...] = x_ref[...] + _rms(m_ref[...], gn_ref[...])
    yr_prev[...] = yr_cur[...]


def _resident(stacked, layer):
    shape = stacked.shape[1:]
    return pl.BlockSpec((None,) + shape, lambda i: (layer,) + (0,) * len(shape),
                        pipeline_mode=pl.Buffered(1))


def _ret_merge(x, proj, ya, consts, wr, wa, wo, g, layer, seq, tc=512):
    t = x.shape[0]
    n_blocks = t // tc
    dm, xi, ze, dec = consts
    qk_w = RET_HEADS * RET_DK
    v_w = RET_HEADS * RET_DV
    colb = MAIN_COLB

    def ret_blk(width, col):
        return pl.BlockSpec((tc, width), lambda s: (jnp.minimum(s, n_blocks - 1), col))

    def merge_blk(width, col):
        return pl.BlockSpec((tc, width), lambda s: (jnp.maximum(s - 1, 0), col))

    def const_spec(a):
        return pl.BlockSpec(a.shape, lambda s: (0,) * a.ndim)

    return pl.pallas_call(
        functools.partial(_ret_merge_kernel, n_chunks=tc // RET_C,
                          blocks_per_seq=seq // tc),
        grid=(n_blocks + 1,),
        in_specs=[
            ret_blk(qk_w, _RQ * colb // qk_w), ret_blk(qk_w, _RK * colb // qk_w),
            ret_blk(v_w, _RV * colb // v_w), ret_blk(v_w, _RG * colb // v_w),
            const_spec(dm), const_spec(xi), const_spec(ze), const_spec(dec),
            merge_blk(D_MODEL, 0), merge_blk(ya.shape[1], 0),
            merge_blk(D_MODEL, _GR * colb // D_MODEL), merge_blk(D_MODEL, _GA * colb // D_MODEL),
            _resident(wr, layer), _resident(wa, layer), _resident(wo, layer),
            _resident(g, layer),
        ],
        out_specs=merge_blk(D_MODEL, 0),
        out_shape=jax.ShapeDtypeStruct((t, D_MODEL), _F32),
        scratch_shapes=[
            pltpu.VMEM((RET_HEADS, RET_DK, RET_DV), _F32),
            pltpu.VMEM((tc, v_w), _BF16),
            pltpu.VMEM((tc, v_w), _BF16),
            pltpu.VMEM((tc, D_MODEL), _BF16),
            pltpu.VMEM((tc, D_MODEL), _F32),
        ],
        compiler_params=pltpu.CompilerParams(
            dimension_semantics=("arbitrary",), vmem_limit_bytes=VMEM_LIMIT),
        name="ret_merge",
    )(proj, proj, proj, proj, dm, xi, ze, dec, x, ya, proj, proj, wr, wa, wo, g)


def _ffn_kernel(x_ref, gpre_ref, wg_ref, wu_ref, wd_ref, gpost_ref, o_ref, a_ref, *, bounds):
    for r0 in range(0, x_ref.shape[0], FFN_ROWS):
        rows = slice(r0, r0 + FFN_ROWS)
        x = x_ref[rows, :]
        h = _rms(x, gpre_ref[...]).astype(_BF16)
        for lo, hi in zip(bounds[:-1], bounds[1:]):
            cols = slice(lo, hi)
            gate = _dot(h, wg_ref[:, cols])
            up = _dot(h, wu_ref[:, cols])
            a_ref[rows, cols] = (gate * jax.nn.sigmoid(gate) * up).astype(_BF16)
        f = _dot(a_ref[rows, :], wd_ref[...])
        o_ref[rows, :] = x + _rms(f, gpost_ref[...])


def _ffn(x, gpre, wg, wu, wd, gpost, layer, tm=1024):
    t = x.shape[0]
    bounds = (0, 3 * MXU_W, 6 * MXU_W, 9 * MXU_W, D_FF)
    return pl.pallas_call(
        functools.partial(_ffn_kernel, bounds=bounds),
        grid=(t // tm,),
        in_specs=[
            pl.BlockSpec((tm, D_MODEL), lambda i: (i, 0)),
            _resident(gpre, layer), _resident(wg, layer), _resident(wu, layer),
            _resident(wd, layer), _resident(gpost, layer),
        ],
        out_specs=pl.BlockSpec((tm, D_MODEL), lambda i: (i, 0)),
        out_shape=jax.ShapeDtypeStruct((t, D_MODEL), _F32),
        scratch_shapes=[pltpu.VMEM((tm, D_FF), _BF16)],
        compiler_params=pltpu.CompilerParams(
            dimension_semantics=("arbitrary",), vmem_limit_bytes=VMEM_LIMIT),
        name="ffn",
    )(x, gpre, wg, wu, wd, gpost)


def _rotary_tables(seq):
    pos = np.arange(seq, dtype=np.float64)[:, None]

    def cos_sin(theta, half):
        ang = pos * (theta ** (-np.arange(half, dtype=np.float64) / half))[None, :]
        return np.cos(ang), np.sin(ang)

    cos_r, sin_r = cos_sin(RET_THETA, RET_DK // 2)
    cos, sin = cos_sin(ATT_THETA, ATT_ROT_HALF)
    rest = LANES - 2 * ATT_ROT_HALF
    cos_a = np.concatenate([cos, cos, np.ones((seq, rest))], axis=-1)
    sin_a = np.concatenate([-sin, sin, np.zeros((seq, rest))], axis=-1)
    return tuple(t.astype(np.float32) for t in (cos_r, sin_r, cos_a, sin_a))


def _retention_consts():
    c = RET_C
    lg = np.log(1.0 - 2.0 ** (-5.0 - np.arange(RET_HEADS, dtype=np.float64)))
    pos = np.arange(c, dtype=np.float64)
    rel = pos[:, None] - pos[None, :]
    dmask = np.where(rel[None] >= 0,
                     np.exp(np.maximum(rel, 0.0)[None] * lg[:, None, None]), 0.0)
    xi = np.exp((pos + 1.0)[None, :] * lg[:, None])[:, :, None]
    zeta = np.exp((c - 1.0 - pos)[None, :] * lg[:, None])[:, :, None]
    decay = np.broadcast_to(np.exp(c * lg)[:, None, None], (RET_HEADS, 1, RET_DV))
    return tuple(np.ascontiguousarray(t, dtype=np.float32) for t in (dmask, xi, zeta, decay))


def kernel(x, w_in, w_ret_out, w_att_out, w_o, w_ffn_gate, w_ffn_up, w_ffn_down,
           g_pre_mix, g_post_mix, g_pre_ffn, g_post_ffn):
    batch, seq, d = x.shape
    depth = w_in.shape[0]
    assert d == D_MODEL and seq % TILE == 0
    cos_r, sin_r, cos_a, sin_a = _rotary_tables(seq)
    consts = _retention_consts()
    xf = x.reshape(batch * seq, d)
    w_in, w_ffn_gate, w_ffn_up, w_ffn_down = (
        w.astype(_BF16) for w in (w_in, w_ffn_gate, w_ffn_up, w_ffn_down))
    g_pre_mix, g_post_mix, g_pre_ffn, g_post_ffn = (
        g[:, None, :] for g in (g_pre_mix, g_post_mix, g_pre_ffn, g_post_ffn))
    for l in range(depth):
        proj, h = _proj_main(xf, g_pre_mix, w_in, l, cos_r, sin_r)
        att = _proj_att(h, w_in, l, cos_a, sin_a)
        ya = _dilated(att, batch, seq)
        xf = _ret_merge(xf, proj, ya, consts, w_ret_out, w_att_out, w_o, g_post_mix, l, seq)
        xf = _ffn(xf, g_pre_ffn, w_ffn_gate, w_ffn_up, w_ffn_down, g_post_ffn, l)
    return xf.reshape(batch, seq, d)
```

```python
import functools
import math

import jax
import jax.numpy as jnp
import numpy as np
from jax import lax
from jax.experimental import pallas as pl
from jax.experimental.pallas import tpu as pltpu

D_MODEL = 1024
RET_HEADS = 4
RET_DK = 256
RET_DV = 512
RET_THETA = 10000.0
ATT_GROUPS = ((128, 1), (512, 4), (2048, 16))
ATT_HPG = 4
ATT_DH = 128
ATT_ROT_HALF = 16
ATT_THETA = 500000.0
D_FF = 2816
NORM_EPS = 1e-6
NEG = -1e30

LANES = 128
MXU_W = 256
ATT_BLK = 128
TILE = 2048
SUB = 512
RET_C = 256
MERGE_PIECES = 4
FFN_ROWS = 512
VMEM_LIMIT = 60 * 1024 * 1024

MAIN_W = 8192
MAIN_COLB = 1024
_RQ, _RK, _RV, _RG, _GR, _GA = 0, 1, 2, 4, 6, 7
ATT_GW = ATT_HPG * ATT_DH
ATT_K0, ATT_V0, ATT_Q0 = 0, ATT_GW, 2 * ATT_GW
ATT_COL_LO = 2 * RET_HEADS * RET_DK + 2 * RET_HEADS * RET_DV
GATE_COL_LO = ATT_COL_LO + 3 * len(ATT_GROUPS) * ATT_GW

_F32 = jnp.float32
_BF16 = jnp.bfloat16


def _dot(a, b):
    return jnp.dot(a, b, preferred_element_type=_F32)


def _dot_nt(a, b):
    return lax.dot_general(a, b, (((1,), (1,)), ((), ())), preferred_element_type=_F32)


def _dot_tn(a, b):
    return lax.dot_general(a, b, (((0,), (0,)), ((), ())), preferred_element_type=_F32)


def _res16(sub, blk):
    return 4 * blk + sub


def _rms(x, g):
    return x * lax.rsqrt(jnp.mean(x * x, axis=-1, keepdims=True) + NORM_EPS) * g


def _proj_main_kernel(x0_ref, x1_ref, x2_ref, x3_ref, g_ref, w_ref, wlo_ref, whi_ref,
                      cr_ref, sr_ref, o_ref, h_ref):
    j = pl.program_id(1)
    x_refs = (x0_ref, x1_ref, x2_ref, x3_ref)

    def norm_chunk(c):
        for r in range(0, SUB, 256):
            h_ref[c * SUB + r:c * SUB + r + 256, :] = _rms(
                x_refs[c][r:r + 256, :], g_ref[...]).astype(_BF16)

    def run(epilogue, prologue=None):
        w = w_ref[...].astype(_BF16)
        for c in range(TILE // SUB):
            if prologue is not None:
                prologue(c)
            epilogue(_dot(h_ref[c * SUB:(c + 1) * SUB, :], w), c)

    def plain(acc, c):
        o_ref[c * SUB:(c + 1) * SUB, :] = acc.astype(_BF16)

    def run_halves():
        half = MAIN_COLB // 2
        for c in range(TILE // SUB):
            lhs = h_ref[c * SUB:(c + 1) * SUB, :]
            o_ref[c * SUB:(c + 1) * SUB, :half] = (
                _dot(lhs, wlo_ref[...].astype(_BF16)).astype(_BF16))
            o_ref[c * SUB:(c + 1) * SUB, half:] = (
                _dot(lhs, whi_ref[...].astype(_BF16)).astype(_BF16))

    def ret_rotary(scale):
        def f(acc, c):
            for bi in range(SUB // ATT_BLK):
                lo = c * SUB + bi * ATT_BLK
                cos = cr_ref[lo:lo + ATT_BLK, :]
                sin = sr_ref[lo:lo + ATT_BLK, :]
                a = acc[bi * ATT_BLK:(bi + 1) * ATT_BLK, :]
                outs = []
                for hh in range(MAIN_COLB // RET_DK):
                    x1 = a[:, hh * RET_DK:hh * RET_DK + LANES]
                    x2 = a[:, hh * RET_DK + LANES:(hh + 1) * RET_DK]
                    outs.append((x1 * cos - x2 * sin) * scale)
                    outs.append((x1 * sin + x2 * cos) * scale)
                o_ref[lo:lo + ATT_BLK, :] = jnp.concatenate(outs, axis=-1).astype(_BF16)
        return f

    @pl.when(j == _RQ)
    def _():
        run(ret_rotary(1.0), prologue=norm_chunk)

    @pl.when(j == _RK)
    def _():
        run(ret_rotary(RET_DK ** -0.5))

    @pl.when((j >= _RV) & (j < _GR))
    def _():
        run(plain)

    @pl.when(j >= _GR)
    def _():
        run_halves()


def _proj_main(x, g, w_in, layer, cos_r, sin_r):
    t = x.shape[0]
    n_tiles = t // TILE
    n_col = MAIN_W // MAIN_COLB
    tiles_per_seq = cos_r.shape[0] // TILE
    half = MAIN_COLB // 2
    gate_blk = GATE_COL_LO // half

    def tile_from(i, j, switch_at):
        return jnp.minimum(i + (j >= switch_at).astype(jnp.int32), n_tiles - 1)

    def x_chunk(c):
        per = TILE // SUB
        return pl.BlockSpec(
            (SUB, D_MODEL), lambda i, j: (tile_from(i, j, n_col - per + c) * per + c, 0))

    tab_spec = pl.BlockSpec(
        (TILE, LANES), lambda i, j: (tile_from(i, j, _RV) % tiles_per_seq, 0))

    def gate_half(which):
        return pl.BlockSpec(
            (None, D_MODEL, half),
            lambda i, j: (layer, 0, gate_blk + 2 * jnp.maximum(j - _GR, 0) + which))

    return pl.pallas_call(
        _proj_main_kernel,
        grid=(n_tiles, n_col),
        in_specs=[
            x_chunk(0), x_chunk(1), x_chunk(2), x_chunk(3),
            pl.BlockSpec((None, 1, D_MODEL), lambda i, j: (layer, 0, 0)),
            pl.BlockSpec((None, D_MODEL, MAIN_COLB),
                         lambda i, j: (layer, 0, jnp.minimum(j, _GR - 1))),
            gate_half(0), gate_half(1),
            tab_spec, tab_spec,
        ],
        out_specs=[
            pl.BlockSpec((TILE, MAIN_COLB), lambda i, j: (i, j)),
            pl.BlockSpec((TILE, D_MODEL), lambda i, j: (i, 0)),
        ],
        out_shape=[
            jax.ShapeDtypeStruct((t, MAIN_W), _BF16),
            jax.ShapeDtypeStruct((t, D_MODEL), _BF16),
        ],
        compiler_params=pltpu.CompilerParams(
            dimension_semantics=("arbitrary", "arbitrary"),
            vmem_limit_bytes=VMEM_LIMIT),
        name="proj_main",
    )(x, x, x, x, g, w_in, w_in, w_in, cos_r, sin_r)


def _proj_att_kernel(h_ref, wq_ref, wk_ref, wv_ref, ca_ref, sa_ref, o_ref,
                     hs_ref, hq_ref, hp_ref):
    g = pl.program_id(1)
    n_slab = D_MODEL // LANES
    n_blk = TILE // ATT_BLK

    def build_slabs():
        for c in range(TILE // 256):
            y = h_ref[c * 256:(c + 1) * 256, :].astype(_F32)
            for k in range(n_slab):
                hs_ref[k, c * 256:(c + 1) * 256, :] = y[:, k * LANES:(k + 1) * LANES]

    def permute4():
        for b in range(n_blk):
            start = (b // 4) * (4 * ATT_BLK) + (b % 4)
            for k in range(n_slab):
                hp_ref[b * ATT_BLK:(b + 1) * ATT_BLK, k * LANES:(k + 1) * LANES] = (
                    hs_ref[k, pl.ds(start, ATT_BLK, stride=4), :].astype(_BF16))

    def permute16():
        quarter = TILE // 4
        for p0 in range(4):
            for k in range(n_slab):
                hq_ref[k] = hs_ref[k, pl.ds(p0, quarter, stride=4), :]
            for p1 in range(4):
                lo = (p0 * 4 + p1) * ATT_BLK
                for k in range(n_slab):
                    hp_ref[lo:lo + ATT_BLK, k * LANES:(k + 1) * LANES] = (
                        hq_ref[k, pl.ds(p1, ATT_BLK, stride=4), :].astype(_BF16))

    def table_rows(t_ref, c, bi, r):
        if r == 1:
            return t_ref[c * SUB + bi * ATT_BLK:c * SUB + (bi + 1) * ATT_BLK, :]
        if r == 4:
            return t_ref[pl.ds(c * SUB + bi, ATT_BLK, stride=4), :]
        return t_ref[pl.ds(_res16(c, bi), ATT_BLK, stride=16), :]

    lane = lax.broadcasted_iota(jnp.int32, (ATT_BLK, LANES), 1)

    def run(lhs_ref, r):
        for c in range(TILE // SUB):
            lhs = lhs_ref[c * SUB:(c + 1) * SUB, :]
            for col0, w_ref in ((ATT_Q0, wq_ref), (ATT_K0, wk_ref)):
                acc = _dot(lhs, w_ref[...].astype(_BF16))
                for bi in range(SUB // ATT_BLK):
                    lo = c * SUB + bi * ATT_BLK
                    cos = table_rows(ca_ref, c, bi, r)
                    sin = table_rows(sa_ref, c, bi, r)
                    outs = []
                    for hh in range(ATT_HPG):
                        x = acc[bi * ATT_BLK:(bi + 1) * ATT_BLK, hh * ATT_DH:(hh + 1) * ATT_DH]
                        partner = jnp.where(lane < ATT_ROT_HALF,
                                            pltpu.roll(x, LANES - ATT_ROT_HALF, 1),
                                            pltpu.roll(x, ATT_ROT_HALF, 1))
                        outs.append(x * cos + partner * sin)
                    o_ref[lo:lo + ATT_BLK, col0:col0 + ATT_GW] = (
                        jnp.concatenate(outs, axis=-1).astype(_BF16))
            o_ref[c * SUB:(c + 1) * SUB, ATT_V0:ATT_V0 + ATT_GW] = (
                _dot(lhs, wv_ref[...].astype(_BF16)).astype(_BF16))

    @pl.when(g == 0)
    def _():
        build_slabs()
        run(h_ref, 1)

    @pl.when(g == 1)
    def _():
        permute4()
        run(hp_ref, 4)

    @pl.when(g == 2)
    def _():
        permute16()
        run(hp_ref, 16)


def _proj_att(h, w_in, layer, cos_a, sin_a):
    t = h.shape[0]
    n_g = len(ATT_GROUPS)
    tiles_per_seq = cos_a.shape[0] // TILE
    tab_spec = pl.BlockSpec((TILE, LANES), lambda i, g: (i % tiles_per_seq, 0))
    att_blk = ATT_COL_LO // ATT_GW

    def w_spec(which):
        return pl.BlockSpec((None, D_MODEL, ATT_GW),
                            lambda i, g: (layer, 0, att_blk + which * n_g + g))

    return pl.pallas_call(
        _proj_att_kernel,
        grid=(t // TILE, n_g),
        in_specs=[
            pl.BlockSpec((TILE, D_MODEL), lambda i, g: (
                jnp.minimum(i + (g >= 1).astype(jnp.int32), t // TILE - 1), 0)),
            w_spec(0), w_spec(1), w_spec(2),
            tab_spec, tab_spec,
        ],
        out_specs=pl.BlockSpec((None, TILE, 3 * ATT_GW), lambda i, g: (g, i, 0)),
        out_shape=jax.ShapeDtypeStruct((n_g, t, 3 * ATT_GW), _BF16),
        scratch_shapes=[
            pltpu.VMEM((D_MODEL // LANES, TILE, LANES), _F32),
            pltpu.VMEM((D_MODEL // LANES, TILE // 4, LANES), _F32),
            pltpu.VMEM((TILE, D_MODEL), _BF16),
        ],
        compiler_params=pltpu.CompilerParams(
            dimension_semantics=("arbitrary", "arbitrary"),
            vmem_limit_bytes=VMEM_LIMIT),
        name="proj_att",
    )(h, w_in, w_in, w_in, cos_a, sin_a)


def _dilated_kernel(cur0, prev0, cur1, prev1, cur2, prev2, o_ref, out_ref, lse_ref):
    i = pl.program_id(1)
    ss = pl.program_id(2)
    n_sub = TILE // SUB
    scale = 1.0 / math.sqrt(ATT_DH)
    ci = lax.broadcasted_iota(jnp.int32, (ATT_BLK, ATT_BLK), 0)
    mi = lax.broadcasted_iota(jnp.int32, (ATT_BLK, ATT_BLK), 1)
    prev_band = mi >= ci
    cur_band = mi <= ci

    groups = ((cur0, prev0, 1), (cur1, prev1, 4), (cur2, prev2, 16))

    def item(g, bi):
        cur_ref, prev_ref, r = groups[g]
        rows = slice(bi * ATT_BLK, (bi + 1) * ATT_BLK)
        if r == 1:
            if bi == 0:
                p_ref, prows = prev_ref, slice(0, ATT_BLK)
                has_prev = (i > 0) | (ss > 0)
            else:
                p_ref, prows = cur_ref, slice((bi - 1) * ATT_BLK, bi * ATT_BLK)
                has_prev = None
            dst = pl.ds(pl.multiple_of(ss * SUB + bi * ATT_BLK, ATT_BLK), ATT_BLK)
        elif r == 4:
            p_ref, prows = prev_ref, rows
            has_prev = (i > 0) | (ss > 0)
            dst = pl.ds(ss * SUB + bi, ATT_BLK, stride=4)
        else:
            p_ref, prows = prev_ref, rows
            has_prev = i > 0
            dst = pl.ds(_res16(ss, bi), ATT_BLK, stride=16)
        pmask = prev_band if has_prev is None else (prev_band & has_prev)
        return dict(g=g, cur_ref=cur_ref, p_ref=p_ref, rows=rows, prows=prows,
                    pmask=pmask, dst=dst)

    def head_cols(col0, h):
        return slice(col0 + h * ATT_DH, col0 + (h + 1) * ATT_DH)

    def score_stage(it):
        out = []
        for h in range(ATT_HPG):
            q = it["cur_ref"][it["rows"], head_cols(ATT_Q0, h)]
            out.append((_dot_nt(q, it["p_ref"][it["prows"], head_cols(ATT_K0, h)]),
                        _dot_nt(q, it["cur_ref"][it["rows"], head_cols(ATT_K0, h)])))
        return out

    def softmax_stage(it, scores):
        out = []
        for sp, sc in scores:
            sp = jnp.where(it["pmask"], sp * scale, NEG)
            sc = jnp.where(cur_band, sc * scale, NEG)
            m = jnp.max(jnp.maximum(sp, sc), axis=-1, keepdims=True)
            pp = jnp.exp(sp - m)
            pc = jnp.exp(sc - m)
            l = jnp.sum(pp + pc, axis=-1, keepdims=True)
            out.append((pp.astype(_BF16), pc.astype(_BF16), m, l))
        return out

    def value_stage(it, probs):
        for h, (pp, pc, m, l) in enumerate(probs):
            num = (_dot(pp, it["p_ref"][it["prows"], head_cols(ATT_V0, h)])
                   + _dot(pc, it["cur_ref"][it["rows"], head_cols(ATT_V0, h)]))
            out_ref[it["g"], h, it["dst"], :] = num * (1.0 / l)
            lse_ref[it["g"], h, it["dst"], :] = jnp.broadcast_to(m + jnp.log(l),
                                                                 (ATT_BLK, LANES))

    items = [item(g, bi) for g in range(len(groups)) for bi in range(SUB // ATT_BLK)]
    scores = score_stage(items[0])
    for n, it in enumerate(items):
        nxt = score_stage(items[n + 1]) if n + 1 < len(items) else None
        value_stage(it, softmax_stage(it, scores))
        scores = nxt

    @pl.when(ss == n_sub - 1)
    def _():
        n_g = len(ATT_GROUPS)

        def body(c, carry):
            rows = pl.ds(pl.multiple_of(c * ATT_BLK, ATT_BLK), ATT_BLK)
            outs = []
            for h in range(ATT_HPG):
                lses = [lse_ref[g, h, rows, :] for g in range(n_g)]
                top = functools.reduce(jnp.maximum, lses)
                ws = [jnp.exp(x - top) for x in lses]
                inv = 1.0 / sum(ws)
                outs.append(sum(ws[g] * out_ref[g, h, rows, :] for g in range(n_g)) * inv)
            o_ref[rows, :] = jnp.concatenate(outs, axis=-1).astype(_BF16)
            return carry
        lax.fori_loop(0, TILE // ATT_BLK, body, 0)


def _dilated(att, batch, seq):
    tiles = seq // TILE
    n_sub = TILE // SUB
    subs_per_seq = seq // SUB
    per = SUB // ATT_BLK

    def cur(g):
        return pl.BlockSpec(
            (None, SUB, 3 * ATT_GW),
            lambda b, i, s: (g, b * subs_per_seq + i * n_sub + s, 0))

    def prev_sub(g, back):
        return pl.BlockSpec(
            (None, SUB, 2 * ATT_GW),
            lambda b, i, s: (g, b * subs_per_seq + jnp.maximum(i * n_sub + s - back, 0), 0))

    def prev_blk(g):
        return pl.BlockSpec(
            (None, ATT_BLK, 2 * ATT_GW),
            lambda b, i, s: (g, b * subs_per_seq * per
                             + jnp.maximum((i * n_sub + s) * per - 1, 0), 0))

    in_specs = [cur(0), prev_blk(0), cur(1), prev_sub(1, 1), cur(2), prev_sub(2, n_sub)]

    return pl.pallas_call(
        _dilated_kernel,
        grid=(batch, tiles, n_sub),
        in_specs=in_specs,
        out_specs=pl.BlockSpec((TILE, ATT_GW), lambda b, i, s: (b * tiles + i, 0)),
        out_shape=jax.ShapeDtypeStruct((batch * seq, ATT_GW), _BF16),
        scratch_shapes=[
            pltpu.VMEM((len(ATT_GROUPS), ATT_HPG, TILE, LANES), _F32),
            pltpu.VMEM((len(ATT_GROUPS), ATT_HPG, TILE, LANES), _F32),
        ],
        compiler_params=pltpu.CompilerParams(
            dimension_semantics=("arbitrary", "arbitrary", "arbitrary"),
            vmem_limit_bytes=VMEM_LIMIT),
        name="dilated",
    )(*([att] * 6))


def _ret_merge_kernel(q_ref, k_ref, v_ref, g_ref, dm_ref, xi_ref, ze_ref, dec_ref,
                      x_ref, ya_ref, gr_ref, ga_ref, wr_ref, wa_ref, wo_ref, gn_ref,
                      o_ref, r_ref, yr_prev, yr_cur, mg_ref, m_ref, *, n_chunks, blocks_per_seq):
    s = pl.program_id(0)

    @pl.when(s % blocks_per_seq == 0)
    def _():
        r_ref[...] = jnp.zeros_like(r_ref)

    @pl.when(s == 0)
    def _():
        yr_prev[...] = jnp.zeros_like(yr_prev)

    pw = D_MODEL // MERGE_PIECES

    def branch_piece(p):
        cols = slice(p * pw, (p + 1) * pw)
        y_ret = _dot(yr_prev[...], wr_ref[:, cols].astype(_BF16))
        y_att = _dot(ya_ref[...], wa_ref[:, cols].astype(_BF16))
        mg_ref[:, cols] = (jax.nn.sigmoid(gr_ref[:, cols].astype(_F32)) * y_ret
                           + jax.nn.sigmoid(ga_ref[:, cols].astype(_F32)) * y_att
                           ).astype(_BF16)

    def out_piece(p):
        cols = slice(p * pw, (p + 1) * pw)
        m_ref[:, cols] = _dot(mg_ref[...], wo_ref[:, cols].astype(_BF16))

    def retention_unit(c, h):
        rows = slice(c * RET_C, (c + 1) * RET_C)
        q = q_ref[rows, h * RET_DK:(h + 1) * RET_DK]
        k = k_ref[rows, h * RET_DK:(h + 1) * RET_DK]
        v = v_ref[rows, h * RET_DV:(h + 1) * RET_DV]
        scores = _dot_nt(q, k) * dm_ref[h]
        qx = (q.astype(_F32) * xi_ref[h]).astype(_BF16)
        r_old = r_ref[h]
        y = _dot(jnp.concatenate([scores.astype(_BF16), qx], axis=1),
                 jnp.concatenate([v, r_old.astype(_BF16)], axis=0))
        kz = (k.astype(_F32) * ze_ref[h]).astype(_BF16)
        r_ref[h] = dec_ref[h] * r_old + _dot_tn(kz, v)
        mu = jnp.mean(y, axis=-1, keepdims=True)
        yc = y - mu
        var = jnp.mean(yc * yc, axis=-1, keepdims=True)
        yn = yc * lax.rsqrt(var + NORM_EPS)
        gate = g_ref[rows, h * RET_DV:(h + 1) * RET_DV].astype(_F32)
        yr_cur[rows, h * RET_DV:(h + 1) * RET_DV] = (
            gate * jax.nn.sigmoid(gate) * yn).astype(_BF16)

    units = [(c, h) for c in range(n_chunks) for h in range(RET_HEADS)]
    assert len(units) == 2 * MERGE_PIECES == 8
    order = "R B R R B R B R R B R O O R O O".split()
    nxt = {"R": iter(units), "B": iter(range(MERGE_PIECES)), "O": iter(range(MERGE_PIECES))}
    for kind in order:
        if kind == "R":
            retention_unit(*next(nxt["R"]))
        elif kind == "B":
            branch_piece(next(nxt["B"]))
        else:
            out_piece(next(nxt["O"]))
    o_ref[...] = x_ref[...] + _rms(m_ref[...], gn_ref[...])
    yr_prev[...] = yr_cur[...]


def _resident(stacked, layer):
    shape = stacked.shape[1:]
    return pl.BlockSpec((None,) + shape, lambda i: (layer,) + (0,) * len(shape),
                        pipeline_mode=pl.Buffered(1))


def _ret_merge(x, proj, ya, consts, wr, wa, wo, g, layer, seq, tc=512):
    t = x.shape[0]
    n_blocks = t // tc
    dm, xi, ze, dec = consts
    qk_w = RET_HEADS * RET_DK
    v_w = RET_HEADS * RET_DV
    colb = MAIN_COLB

    def ret_blk(width, col):
        return pl.BlockSpec((tc, width), lambda s: (jnp.minimum(s, n_blocks - 1), col))

    def merge_blk(width, col):
        return pl.BlockSpec((tc, width), lambda s: (jnp.maximum(s - 1, 0), col))

    def const_spec(a):
        return pl.BlockSpec(a.shape, lambda s: (0,) * a.ndim)

    return pl.pallas_call(
        functools.partial(_ret_merge_kernel, n_chunks=tc // RET_C,
                          blocks_per_seq=seq // tc),
        grid=(n_blocks + 1,),
        in_specs=[
            ret_blk(qk_w, _RQ * colb // qk_w), ret_blk(qk_w, _RK * colb // qk_w),
            ret_blk(v_w, _RV * colb // v_w), ret_blk(v_w, _RG * colb // v_w),
            const_spec(dm), const_spec(xi), const_spec(ze), const_spec(dec),
            merge_blk(D_MODEL, 0), merge_blk(ya.shape[1], 0),
            merge_blk(D_MODEL, _GR * colb // D_MODEL), merge_blk(D_MODEL, _GA * colb // D_MODEL),
            _resident(wr, layer), _resident(wa, layer), _resident(wo, layer),
            _resident(g, layer),
        ],
        out_specs=merge_blk(D_MODEL, 0),
        out_shape=jax.ShapeDtypeStruct((t, D_MODEL), _F32),
        scratch_shapes=[
            pltpu.VMEM((RET_HEADS, RET_DK, RET_DV), _F32),
            pltpu.VMEM((tc, v_w), _BF16),
            pltpu.VMEM((tc, v_w), _BF16),
            pltpu.VMEM((tc, D_MODEL), _BF16),
            pltpu.VMEM((tc, D_MODEL), _F32),
        ],
        compiler_params=pltpu.CompilerParams(
            dimension_semantics=("arbitrary",), vmem_limit_bytes=VMEM_LIMIT),
        name="ret_merge",
    )(proj, proj, proj, proj, dm, xi, ze, dec, x, ya, proj, proj, wr, wa, wo, g)


def _ffn_kernel(x_ref, gpre_ref, wg_ref, wu_ref, wd_ref, gpost_ref, o_ref, a_ref, *, bounds):
    for r0 in range(0, x_ref.shape[0], FFN_ROWS):
        rows = slice(r0, r0 + FFN_ROWS)
        x = x_ref[rows, :]
        h = _rms(x, gpre_ref[...]).astype(_BF16)
        for lo, hi in zip(bounds[:-1], bounds[1:]):
            cols = slice(lo, hi)
            gate = _dot(h, wg_ref[:, cols])
            up = _dot(h, wu_ref[:, cols])
            a_ref[rows, cols] = (gate * jax.nn.sigmoid(gate) * up).astype(_BF16)
        f = _dot(a_ref[rows, :], wd_ref[...])
        o_ref[rows, :] = x + _rms(f, gpost_ref[...])


def _ffn(x, gpre, wg, wu, wd, gpost, layer, tm=1024):
    t = x.shape[0]
    bounds = (0, 3 * MXU_W, 6 * MXU_W, 9 * MXU_W, D_FF)
    return pl.pallas_call(
        functools.partial(_ffn_kernel, bounds=bounds),
        grid=(t // tm,),
        in_specs=[
            pl.BlockSpec((tm, D_MODEL), lambda i: (i, 0)),
            _resident(gpre, layer), _resident(wg, layer), _resident(wu, layer),
            _resident(wd, layer), _resident(gpost, layer),
        ],
        out_specs=pl.BlockSpec((tm, D_MODEL), lambda i: (i, 0)),
        out_shape=jax.ShapeDtypeStruct((t, D_MODEL), _F32),
        scratch_shapes=[pltpu.VMEM((tm, D_FF), _BF16)],
        compiler_params=pltpu.CompilerParams(
            dimension_semantics=("arbitrary",), vmem_limit_bytes=VMEM_LIMIT),
        name="ffn",
    )(x, gpre, wg, wu, wd, gpost)


def _rotary_tables(seq):
    pos = np.arange(seq, dtype=np.float64)[:, None]

    def cos_sin(theta, half):
        ang = pos * (theta ** (-np.arange(half, dtype=np.float64) / half))[None, :]
        return np.cos(ang), np.sin(ang)

    cos_r, sin_r = cos_sin(RET_THETA, RET_DK // 2)
    cos, sin = cos_sin(ATT_THETA, ATT_ROT_HALF)
    rest = LANES - 2 * ATT_ROT_HALF
    cos_a = np.concatenate([cos, cos, np.ones((seq, rest))], axis=-1)
    sin_a = np.concatenate([-sin, sin, np.zeros((seq, rest))], axis=-1)
    return tuple(t.astype(np.float32) for t in (cos_r, sin_r, cos_a, sin_a))


def _retention_consts():
    c = RET_C
    lg = np.log(1.0 - 2.0 ** (-5.0 - np.arange(RET_HEADS, dtype=np.float64)))
    pos = np.arange(c, dtype=np.float64)
    rel = pos[:, None] - pos[None, :]
    dmask = np.where(rel[None] >= 0,
                     np.exp(np.maximum(rel, 0.0)[None] * lg[:, None, None]), 0.0)
    xi = np.exp((pos + 1.0)[None, :] * lg[:, None])[:, :, None]
    zeta = np.exp((c - 1.0 - pos)[None, :] * lg[:, None])[:, :, None]
    decay = np.broadcast_to(np.exp(c * lg)[:, None, None], (RET_HEADS, 1, RET_DV))
    return tuple(np.ascontiguousarray(t, dtype=np.float32) for t in (dmask, xi, zeta, decay))


def kernel(x, w_in, w_ret_out, w_att_out, w_o, w_ffn_gate, w_ffn_up, w_ffn_down,
           g_pre_mix, g_post_mix, g_pre_ffn, g_post_ffn):
    batch, seq, d = x.shape
    depth = w_in.shape[0]
    assert d == D_MODEL and seq % TILE == 0
    cos_r, sin_r, cos_a, sin_a = _rotary_tables(seq)
    consts = _retention_consts()
    xf = x.reshape(batch * seq, d)
    w_ffn_gate, w_ffn_up, w_ffn_down = (
        w.astype(_BF16) for w in (w_ffn_gate, w_ffn_up, w_ffn_down))
    g_pre_mix, g_post_mix, g_pre_ffn, g_post_ffn = (
        g[:, None, :] for g in (g_pre_mix, g_post_mix, g_pre_ffn, g_post_ffn))
    for l in range(depth):
        proj, h = _proj_main(xf, g_pre_mix, w_in, l, cos_r, sin_r)
        att = _proj_att(h, w_in, l, cos_a, sin_a)
        ya = _dilated(att, batch, seq)
        xf = _ret_merge(xf, proj, ya, consts, w_ret_out, w_att_out, w_o, g_post_mix, l, seq)
        xf = _ffn(xf, g_pre_ffn, w_ffn_gate, w_ffn_up, w_ffn_down, g_post_ffn, l)
    return xf.reshape(batch, seq, d)
```

```python
import functools
import math

import jax
import jax.numpy as jnp
import numpy as np
from jax import lax
from jax.experimental import pallas as pl
from jax.experimental.pallas import tpu as pltpu

D_MODEL = 1024
RET_HEADS = 4
RET_DK = 256
RET_DV = 512
RET_THETA = 10000.0
ATT_GROUPS = ((128, 1), (512, 4), (2048, 16))
ATT_HPG = 4
ATT_DH = 128
ATT_ROT_HALF = 16
ATT_THETA = 500000.0
D_FF = 2816
NORM_EPS = 1e-6
NEG = -1e30

LANES = 128
MXU_W = 256
ATT_BLK = 128
TILE = 2048
SUB = 512
RET_C = 256
MERGE_PIECES = 4
FFN_ROWS = 256
VMEM_LIMIT = 60 * 1024 * 1024

MAIN_W = 8192
MAIN_COLB = 1024
_RQ, _RK, _RV, _RG, _GR, _GA = 0, 1, 2, 4, 6, 7
ATT_GW = ATT_HPG * ATT_DH
ATT_K0, ATT_V0, ATT_Q0 = 0, ATT_GW, 2 * ATT_GW
ATT_COL_LO = 2 * RET_HEADS * RET_DK + 2 * RET_HEADS * RET_DV
GATE_COL_LO = ATT_COL_LO + 3 * len(ATT_GROUPS) * ATT_GW

_F32 = jnp.float32
_BF16 = jnp.bfloat16


def _dot(a, b):
    return jnp.dot(a, b, preferred_element_type=_F32)


def _dot_nt(a, b):
    return lax.dot_general(a, b, (((1,), (1,)), ((), ())), preferred_element_type=_F32)


def _dot_tn(a, b):
    return lax.dot_general(a, b, (((0,), (0,)), ((), ())), preferred_element_type=_F32)


def _res16(sub, blk):
    return 4 * blk + sub


def _rms(x, g):
    return x * lax.rsqrt(jnp.mean(x * x, axis=-1, keepdims=True) + NORM_EPS) * g


def _proj_main_kernel(x0_ref, x1_ref, x2_ref, x3_ref, g_ref, w_ref, wlo_ref, whi_ref,
                      cr_ref, sr_ref, o_ref, h_ref):
    j = pl.program_id(1)
    x_refs = (x0_ref, x1_ref, x2_ref, x3_ref)

    def norm_chunk(c):
        for r in range(0, SUB, 256):
            h_ref[c * SUB + r:c * SUB + r + 256, :] = _rms(
                x_refs[c][r:r + 256, :], g_ref[...]).astype(_BF16)

    def run(epilogue, prologue=None):
        w = w_ref[...].astype(_BF16)
        for c in range(TILE // SUB):
            if prologue is not None:
                prologue(c)
            epilogue(_dot(h_ref[c * SUB:(c + 1) * SUB, :], w), c)

    def plain(acc, c):
        o_ref[c * SUB:(c + 1) * SUB, :] = acc.astype(_BF16)

    def run_halves():
        half = MAIN_COLB // 2
        for c in range(TILE // SUB):
            lhs = h_ref[c * SUB:(c + 1) * SUB, :]
            o_ref[c * SUB:(c + 1) * SUB, :half] = (
                _dot(lhs, wlo_ref[...].astype(_BF16)).astype(_BF16))
            o_ref[c * SUB:(c + 1) * SUB, half:] = (
                _dot(lhs, whi_ref[...].astype(_BF16)).astype(_BF16))

    def ret_rotary(scale):
        def f(acc, c):
            for bi in range(SUB // ATT_BLK):
                lo = c * SUB + bi * ATT_BLK
                cos = cr_ref[lo:lo + ATT_BLK, :]
                sin = sr_ref[lo:lo + ATT_BLK, :]
                a = acc[bi * ATT_BLK:(bi + 1) * ATT_BLK, :]
                outs = []
                for hh in range(MAIN_COLB // RET_DK):
                    x1 = a[:, hh * RET_DK:hh * RET_DK + LANES]
                    x2 = a[:, hh * RET_DK + LANES:(hh + 1) * RET_DK]
                    outs.append((x1 * cos - x2 * sin) * scale)
                    outs.append((x1 * sin + x2 * cos) * scale)
                o_ref[lo:lo + ATT_BLK, :] = jnp.concatenate(outs, axis=-1).astype(_BF16)
        return f

    @pl.when(j == _RQ)
    def _():
        run(ret_rotary(1.0), prologue=norm_chunk)

    @pl.when(j == _RK)
    def _():
        run(ret_rotary(RET_DK ** -0.5))

    @pl.when((j >= _RV) & (j < _GR))
    def _():
        run(plain)

    @pl.when(j >= _GR)
    def _():
        run_halves()


def _proj_main(x, g, w_in, layer, cos_r, sin_r):
    t = x.shape[0]
    n_tiles = t // TILE
    n_col = MAIN_W // MAIN_COLB
    tiles_per_seq = cos_r.shape[0] // TILE
    half = MAIN_COLB // 2
    gate_blk = GATE_COL_LO // half

    def tile_from(i, j, switch_at):
        return jnp.minimum(i + (j >= switch_at).astype(jnp.int32), n_tiles - 1)

    def x_chunk(c):
        per = TILE // SUB
        return pl.BlockSpec(
            (SUB, D_MODEL), lambda i, j: (tile_from(i, j, n_col - per + c) * per + c, 0))

    tab_spec = pl.BlockSpec(
        (TILE, LANES), lambda i, j: (tile_from(i, j, _RV) % tiles_per_seq, 0))

    def gate_half(which):
        return pl.BlockSpec(
            (None, D_MODEL, half),
            lambda i, j: (layer, 0, gate_blk + 2 * jnp.maximum(j - _GR, 0) + which))

    return pl.pallas_call(
        _proj_main_kernel,
        grid=(n_tiles, n_col),
        in_specs=[
            x_chunk(0), x_chunk(1), x_chunk(2), x_chunk(3),
            pl.BlockSpec((None, 1, D_MODEL), lambda i, j: (layer, 0, 0)),
            pl.BlockSpec((None, D_MODEL, MAIN_COLB),
                         lambda i, j: (layer, 0, jnp.minimum(j, _GR - 1))),
            gate_half(0), gate_half(1),
            tab_spec, tab_spec,
        ],
        out_specs=[
            pl.BlockSpec((TILE, MAIN_COLB), lambda i, j: (i, j)),
            pl.BlockSpec((TILE, D_MODEL), lambda i, j: (i, 0)),
        ],
        out_shape=[
            jax.ShapeDtypeStruct((t, MAIN_W), _BF16),
            jax.ShapeDtypeStruct((t, D_MODEL), _BF16),
        ],
        compiler_params=pltpu.CompilerParams(
            dimension_semantics=("arbitrary", "arbitrary"),
            vmem_limit_bytes=VMEM_LIMIT),
        name="proj_main",
    )(x, x, x, x, g, w_in, w_in, w_in, cos_r, sin_r)


def _proj_att_kernel(h_ref, wq_ref, wk_ref, wv_ref, ca_ref, sa_ref, o_ref,
                     hs_ref, hq_ref, hp_ref):
    g = pl.program_id(1)
    n_slab = D_MODEL // LANES
    n_blk = TILE // ATT_BLK

    def build_slabs():
        for c in range(TILE // 256):
            y = h_ref[c * 256:(c + 1) * 256, :].astype(_F32)
            for k in range(n_slab):
                hs_ref[k, c * 256:(c + 1) * 256, :] = y[:, k * LANES:(k + 1) * LANES]

    def permute4():
        for b in range(n_blk):
            start = (b // 4) * (4 * ATT_BLK) + (b % 4)
            for k in range(n_slab):
                hp_ref[b * ATT_BLK:(b + 1) * ATT_BLK, k * LANES:(k + 1) * LANES] = (
                    hs_ref[k, pl.ds(start, ATT_BLK, stride=4), :].astype(_BF16))

    def permute16():
        quarter = TILE // 4
        for p0 in range(4):
            for k in range(n_slab):
                hq_ref[k] = hs_ref[k, pl.ds(p0, quarter, stride=4), :]
            for p1 in range(4):
                lo = (p0 * 4 + p1) * ATT_BLK
                for k in range(n_slab):
                    hp_ref[lo:lo + ATT_BLK, k * LANES:(k + 1) * LANES] = (
                        hq_ref[k, pl.ds(p1, ATT_BLK, stride=4), :].astype(_BF16))

    def table_rows(t_ref, c, bi, r):
        if r == 1:
            return t_ref[c * SUB + bi * ATT_BLK:c * SUB + (bi + 1) * ATT_BLK, :]
        if r == 4:
            return t_ref[pl.ds(c * SUB + bi, ATT_BLK, stride=4), :]
        return t_ref[pl.ds(_res16(c, bi), ATT_BLK, stride=16), :]

    lane = lax.broadcasted_iota(jnp.int32, (ATT_BLK, LANES), 1)

    def run(lhs_ref, r):
        for c in range(TILE // SUB):
            lhs = lhs_ref[c * SUB:(c + 1) * SUB, :]
            for col0, w_ref in ((ATT_Q0, wq_ref), (ATT_K0, wk_ref)):
                acc = _dot(lhs, w_ref[...].astype(_BF16))
                for bi in range(SUB // ATT_BLK):
                    lo = c * SUB + bi * ATT_BLK
                    cos = table_rows(ca_ref, c, bi, r)
                    sin = table_rows(sa_ref, c, bi, r)
                    outs = []
                    for hh in range(ATT_HPG):
                        x = acc[bi * ATT_BLK:(bi + 1) * ATT_BLK, hh * ATT_DH:(hh + 1) * ATT_DH]
                        partner = jnp.where(lane < ATT_ROT_HALF,
                                            pltpu.roll(x, LANES - ATT_ROT_HALF, 1),
                                            pltpu.roll(x, ATT_ROT_HALF, 1))
                        outs.append(x * cos + partner * sin)
                    o_ref[lo:lo + ATT_BLK, col0:col0 + ATT_GW] = (
                        jnp.concatenate(outs, axis=-1).astype(_BF16))
            o_ref[c * SUB:(c + 1) * SUB, ATT_V0:ATT_V0 + ATT_GW] = (
                _dot(lhs, wv_ref[...].astype(_BF16)).astype(_BF16))

    @pl.when(g == 0)
    def _():
        build_slabs()
        run(h_ref, 1)

    @pl.when(g == 1)
    def _():
        permute4()
        run(hp_ref, 4)

    @pl.when(g == 2)
    def _():
        permute16()
        run(hp_ref, 16)


def _proj_att(h, w_in, layer, cos_a, sin_a):
    t = h.shape[0]
    n_g = len(ATT_GROUPS)
    tiles_per_seq = cos_a.shape[0] // TILE
    tab_spec = pl.BlockSpec((TILE, LANES), lambda i, g: (i % tiles_per_seq, 0))
    att_blk = ATT_COL_LO // ATT_GW

    def w_spec(which):
        return pl.BlockSpec((None, D_MODEL, ATT_GW),
                            lambda i, g: (layer, 0, att_blk + which * n_g + g))

    return pl.pallas_call(
        _proj_att_kernel,
        grid=(t // TILE, n_g),
        in_specs=[
            pl.BlockSpec((TILE, D_MODEL), lambda i, g: (
                jnp.minimum(i + (g >= 1).astype(jnp.int32), t // TILE - 1), 0)),
            w_spec(0), w_spec(1), w_spec(2),
            tab_spec, tab_spec,
        ],
        out_specs=pl.BlockSpec((None, TILE, 3 * ATT_GW), lambda i, g: (g, i, 0)),
        out_shape=jax.ShapeDtypeStruct((n_g, t, 3 * ATT_GW), _BF16),
        scratch_shapes=[
            pltpu.VMEM((D_MODEL // LANES, TILE, LANES), _F32),
            pltpu.VMEM((D_MODEL // LANES, TILE // 4, LANES), _F32),
            pltpu.VMEM((TILE, D_MODEL), _BF16),
        ],
        compiler_params=pltpu.CompilerParams(
            dimension_semantics=("arbitrary", "arbitrary"),
            vmem_limit_bytes=VMEM_LIMIT),
        name="proj_att",
    )(h, w_in, w_in, w_in, cos_a, sin_a)


def _dilated_kernel(cur0, prev0, cur1, prev1, cur2, prev2, o_ref, out_ref, lse_ref):
    i = pl.program_id(1)
    ss = pl.program_id(2)
    n_sub = TILE // SUB
    scale = 1.0 / math.sqrt(ATT_DH)
    ci = lax.broadcasted_iota(jnp.int32, (ATT_BLK, ATT_BLK), 0)
    mi = lax.broadcasted_iota(jnp.int32, (ATT_BLK, ATT_BLK), 1)
    prev_band = mi >= ci
    cur_band = mi <= ci

    groups = ((cur0, prev0, 1), (cur1, prev1, 4), (cur2, prev2, 16))

    def item(g, bi):
        cur_ref, prev_ref, r = groups[g]
        rows = slice(bi * ATT_BLK, (bi + 1) * ATT_BLK)
        if r == 1:
            if bi == 0:
                p_ref, prows = prev_ref, slice(0, ATT_BLK)
                has_prev = (i > 0) | (ss > 0)
            else:
                p_ref, prows = cur_ref, slice((bi - 1) * ATT_BLK, bi * ATT_BLK)
                has_prev = None
            dst = pl.ds(pl.multiple_of(ss * SUB + bi * ATT_BLK, ATT_BLK), ATT_BLK)
        elif r == 4:
            p_ref, prows = prev_ref, rows
            has_prev = (i > 0) | (ss > 0)
            dst = pl.ds(ss * SUB + bi, ATT_BLK, stride=4)
        else:
            p_ref, prows = prev_ref, rows
            has_prev = i > 0
            dst = pl.ds(_res16(ss, bi), ATT_BLK, stride=16)
        pmask = prev_band if has_prev is None else (prev_band & has_prev)
        return dict(g=g, cur_ref=cur_ref, p_ref=p_ref, rows=rows, prows=prows,
                    pmask=pmask, dst=dst)

    def head_cols(col0, h):
        return slice(col0 + h * ATT_DH, col0 + (h + 1) * ATT_DH)

    def score_stage(it):
        out = []
        for h in range(ATT_HPG):
            q = it["cur_ref"][it["rows"], head_cols(ATT_Q0, h)]
            out.append((_dot_nt(q, it["p_ref"][it["prows"], head_cols(ATT_K0, h)]),
                        _dot_nt(q, it["cur_ref"][it["rows"], head_cols(ATT_K0, h)])))
        return out

    def softmax_stage(it, scores):
        out = []
        for sp, sc in scores:
            sp = jnp.where(it["pmask"], sp * scale, NEG)
            sc = jnp.where(cur_band, sc * scale, NEG)
            m = jnp.max(jnp.maximum(sp, sc), axis=-1, keepdims=True)
            pp = jnp.exp(sp - m)
            pc = jnp.exp(sc - m)
            l = jnp.sum(pp + pc, axis=-1, keepdims=True)
            out.append((pp.astype(_BF16), pc.astype(_BF16), m, l))
        return out

    def value_stage(it, probs):
        for h, (pp, pc, m, l) in enumerate(probs):
            num = (_dot(pp, it["p_ref"][it["prows"], head_cols(ATT_V0, h)])
                   + _dot(pc, it["cur_ref"][it["rows"], head_cols(ATT_V0, h)]))
            out_ref[it["g"], h, it["dst"], :] = num * (1.0 / l)
            lse_ref[it["g"], h, it["dst"], :] = jnp.broadcast_to(m + jnp.log(l),
                                                                 (ATT_BLK, LANES))

    items = [item(g, bi) for g in range(len(groups)) for bi in range(SUB // ATT_BLK)]
    scores = score_stage(items[0])
    for n, it in enumerate(items):
        nxt = score_stage(items[n + 1]) if n + 1 < len(items) else None
        value_stage(it, softmax_stage(it, scores))
        scores = nxt

    @pl.when(ss == n_sub - 1)
    def _():
        n_g = len(ATT_GROUPS)

        def body(c, carry):
            rows = pl.ds(pl.multiple_of(c * ATT_BLK, ATT_BLK), ATT_BLK)
            outs = []
            for h in range(ATT_HPG):
                lses = [lse_ref[g, h, rows, :] for g in range(n_g)]
                top = functools.reduce(jnp.maximum, lses)
                ws = [jnp.exp(x - top) for x in lses]
                inv = 1.0 / sum(ws)
                outs.append(sum(ws[g] * out_ref[g, h, rows, :] for g in range(n_g)) * inv)
            o_ref[rows, :] = jnp.concatenate(outs, axis=-1).astype(_BF16)
            return carry
        lax.fori_loop(0, TILE // ATT_BLK, body, 0)


def _dilated(att, batch, seq):
    tiles = seq // TILE
    n_sub = TILE // SUB
    subs_per_seq = seq // SUB
    per = SUB // ATT_BLK

    def cur(g):
        return pl.BlockSpec(
            (None, SUB, 3 * ATT_GW),
            lambda b, i, s: (g, b * subs_per_seq + i * n_sub + s, 0))

    def prev_sub(g, back):
        return pl.BlockSpec(
            (None, SUB, 2 * ATT_GW),
            lambda b, i, s: (g, b * subs_per_seq + jnp.maximum(i * n_sub + s - back, 0), 0))

    def prev_blk(g):
        return pl.BlockSpec(
            (None, ATT_BLK, 2 * ATT_GW),
            lambda b, i, s: (g, b * subs_per_seq * per
                             + jnp.maximum((i * n_sub + s) * per - 1, 0), 0))

    in_specs = [cur(0), prev_blk(0), cur(1), prev_sub(1, 1), cur(2), prev_sub(2, n_sub)]

    return pl.pallas_call(
        _dilated_kernel,
        grid=(batch, tiles, n_sub),
        in_specs=in_specs,
        out_specs=pl.BlockSpec((TILE, ATT_GW), lambda b, i, s: (b * tiles + i, 0)),
        out_shape=jax.ShapeDtypeStruct((batch * seq, ATT_GW), _BF16),
        scratch_shapes=[
            pltpu.VMEM((len(ATT_GROUPS), ATT_HPG, TILE, LANES), _F32),
            pltpu.VMEM((len(ATT_GROUPS), ATT_HPG, TILE, LANES), _F32),
        ],
        compiler_params=pltpu.CompilerParams(
            dimension_semantics=("arbitrary", "arbitrary", "arbitrary"),
            vmem_limit_bytes=VMEM_LIMIT),
        name="dilated",
    )(*([att] * 6))


def _ret_merge_kernel(q_ref, k_ref, v_ref, g_ref, dm_ref, xi_ref, ze_ref, dec_ref,
                      x_ref, ya_ref, gr_ref, ga_ref, wr_ref, wa_ref, wo_ref, gn_ref,
                      o_ref, r_ref, yr_prev, yr_cur, mg_ref, m_ref, *, n_chunks, blocks_per_seq):
    s = pl.program_id(0)

    @pl.when(s % blocks_per_seq == 0)
    def _():
        r_ref[...] = jnp.zeros_like(r_ref)

    @pl.when(s == 0)
    def _():
        yr_prev[...] = jnp.zeros_like(yr_prev)

    pw = D_MODEL // MERGE_PIECES

    def branch_piece(p):
        cols = slice(p * pw, (p + 1) * pw)
        y_ret = _dot(yr_prev[...], wr_ref[:, cols].astype(_BF16))
        y_att = _dot(ya_ref[...], wa_ref[:, cols].astype(_BF16))
        mg_ref[:, cols] = (jax.nn.sigmoid(gr_ref[:, cols].astype(_F32)) * y_ret
                           + jax.nn.sigmoid(ga_ref[:, cols].astype(_F32)) * y_att
                           ).astype(_BF16)

    def out_piece(p):
        cols = slice(p * pw, (p + 1) * pw)
        m_ref[:, cols] = _dot(mg_ref[...], wo_ref[:, cols].astype(_BF16))

    def retention_unit(c, h):
        rows = slice(c * RET_C, (c + 1) * RET_C)
        q = q_ref[rows, h * RET_DK:(h + 1) * RET_DK]
        k = k_ref[rows, h * RET_DK:(h + 1) * RET_DK]
        v = v_ref[rows, h * RET_DV:(h + 1) * RET_DV]
        scores = _dot_nt(q, k) * dm_ref[h]
        qx = (q.astype(_F32) * xi_ref[h]).astype(_BF16)
        r_old = r_ref[h]
        y = _dot(jnp.concatenate([scores.astype(_BF16), qx], axis=1),
                 jnp.concatenate([v, r_old.astype(_BF16)], axis=0))
        kz = (k.astype(_F32) * ze_ref[h]).astype(_BF16)
        r_ref[h] = dec_ref[h] * r_old + _dot_tn(kz, v)
        mu = jnp.mean(y, axis=-1, keepdims=True)
        yc = y - mu
        var = jnp.mean(yc * yc, axis=-1, keepdims=True)
        yn = yc * lax.rsqrt(var + NORM_EPS)
        gate = g_ref[rows, h * RET_DV:(h + 1) * RET_DV].astype(_F32)
        yr_cur[rows, h * RET_DV:(h + 1) * RET_DV] = (
            gate * jax.nn.sigmoid(gate) * yn).astype(_BF16)

    units = [(c, h) for c in range(n_chunks) for h in range(RET_HEADS)]
    assert len(units) == 2 * MERGE_PIECES == 8
    order = "R B R R B R B R R B R O O R O O".split()
    nxt = {"R": iter(units), "B": iter(range(MERGE_PIECES)), "O": iter(range(MERGE_PIECES))}
    for kind in order:
        if kind == "R":
            retention_unit(*next(nxt["R"]))
        elif kind == "B":
            branch_piece(next(nxt["B"]))
        else:
            out_piece(next(nxt["O"]))
    o_ref[...] = x_ref[...] + _rms(m_ref[...], gn_ref[...])
    yr_prev[...] = yr_cur[...]


def _resident(stacked, layer):
    shape = stacked.shape[1:]
    return pl.BlockSpec((None,) + shape, lambda i: (layer,) + (0,) * len(shape),
                        pipeline_mode=pl.Buffered(1))


def _ret_merge(x, proj, ya, consts, wr, wa, wo, g, layer, seq, tc=512):
    t = x.shape[0]
    n_blocks = t // tc
    dm, xi, ze, dec = consts
    qk_w = RET_HEADS * RET_DK
    v_w = RET_HEADS * RET_DV
    colb = MAIN_COLB

    def ret_blk(width, col):
        return pl.BlockSpec((tc, width), lambda s: (jnp.minimum(s, n_blocks - 1), col))

    def merge_blk(width, col):
        return pl.BlockSpec((tc, width), lambda s: (jnp.maximum(s - 1, 0), col))

    def const_spec(a):
        return pl.BlockSpec(a.shape, lambda s: (0,) * a.ndim)

    return pl.pallas_call(
        functools.partial(_ret_merge_kernel, n_chunks=tc // RET_C,
                          blocks_per_seq=seq // tc),
        grid=(n_blocks + 1,),
        in_specs=[
            ret_blk(qk_w, _RQ * colb // qk_w), ret_blk(qk_w, _RK * colb // qk_w),
            ret_blk(v_w, _RV * colb // v_w), ret_blk(v_w, _RG * colb // v_w),
            const_spec(dm), const_spec(xi), const_spec(ze), const_spec(dec),
            merge_blk(D_MODEL, 0), merge_blk(ya.shape[1], 0),
            merge_blk(D_MODEL, _GR * colb // D_MODEL), merge_blk(D_MODEL, _GA * colb // D_MODEL),
            _resident(wr, layer), _resident(wa, layer), _resident(wo, layer),
            _resident(g, layer),
        ],
        out_specs=merge_blk(D_MODEL, 0),
        out_shape=jax.ShapeDtypeStruct((t, D_MODEL), _F32),
        scratch_shapes=[
            pltpu.VMEM((RET_HEADS, RET_DK, RET_DV), _F32),
            pltpu.VMEM((tc, v_w), _BF16),
            pltpu.VMEM((tc, v_w), _BF16),
            pltpu.VMEM((tc, D_MODEL), _BF16),
            pltpu.VMEM((tc, D_MODEL), _F32),
        ],
        compiler_params=pltpu.CompilerParams(
            dimension_semantics=("arbitrary",), vmem_limit_bytes=VMEM_LIMIT),
        name="ret_merge",
    )(proj, proj, proj, proj, dm, xi, ze, dec, x, ya, proj, proj, wr, wa, wo, g)


def _ffn_kernel(x_ref, gpre_ref, wg_ref, wu_ref, wd_ref, gpost_ref, o_ref, a_ref, *, bounds):
    for r0 in range(0, x_ref.shape[0], FFN_ROWS):
        rows = slice(r0, r0 + FFN_ROWS)
        x = x_ref[rows, :]
        h = _rms(x, gpre_ref[...]).astype(_BF16)
        for lo, hi in zip(bounds[:-1], bounds[1:]):
            cols = slice(lo, hi)
            gate = _dot(h, wg_ref[:, cols])
            up = _dot(h, wu_ref[:, cols])
            a_ref[rows, cols] = (gate * jax.nn.sigmoid(gate) * up).astype(_BF16)
        f = _dot(a_ref[rows, :], wd_ref[...])
        o_ref[rows, :] = x + _rms(f, gpost_ref[...])


def _ffn(x, gpre, wg, wu, wd, gpost, layer, tm=1024):
    t = x.shape[0]
    bounds = (0, 3 * MXU_W, 6 * MXU_W, 9 * MXU_W, D_FF)
    return pl.pallas_call(
        functools.partial(_ffn_kernel, bounds=bounds),
        grid=(t // tm,),
        in_specs=[
            pl.BlockSpec((tm, D_MODEL), lambda i: (i, 0)),
            _resident(gpre, layer), _resident(wg, layer), _resident(wu, layer),
            _resident(wd, layer), _resident(gpost, layer),
        ],
        out_specs=pl.BlockSpec((tm, D_MODEL), lambda i: (i, 0)),
        out_shape=jax.ShapeDtypeStruct((t, D_MODEL), _F32),
        scratch_shapes=[pltpu.VMEM((tm, D_FF), _BF16)],
        compiler_params=pltpu.CompilerParams(
            dimension_semantics=("arbitrary",), vmem_limit_bytes=VMEM_LIMIT),
        name="ffn",
    )(x, gpre, wg, wu, wd, gpost)


def _rotary_tables(seq):
    pos = np.arange(seq, dtype=np.float64)[:, None]

    def cos_sin(theta, half):
        ang = pos * (theta ** (-np.arange(half, dtype=np.float64) / half))[None, :]
        return np.cos(ang), np.sin(ang)

    cos_r, sin_r = cos_sin(RET_THETA, RET_DK // 2)
    cos, sin = cos_sin(ATT_THETA, ATT_ROT_HALF)
    rest = LANES - 2 * ATT_ROT_HALF
    cos_a = np.concatenate([cos, cos, np.ones((seq, rest))], axis=-1)
    sin_a = np.concatenate([-sin, sin, np.zeros((seq, rest))], axis=-1)
    return tuple(t.astype(np.float32) for t in (cos_r, sin_r, cos_a, sin_a))


def _retention_consts():
    c = RET_C
    lg = np.log(1.0 - 2.0 ** (-5.0 - np.arange(RET_HEADS, dtype=np.float64)))
    pos = np.arange(c, dtype=np.float64)
    rel = pos[:, None] - pos[None, :]
    dmask = np.where(rel[None] >= 0,
                     np.exp(np.maximum(rel, 0.0)[None] * lg[:, None, None]), 0.0)
    xi = np.exp((pos + 1.0)[None, :] * lg[:, None])[:, :, None]
    zeta = np.exp((c - 1.0 - pos)[None, :] * lg[:, None])[:, :, None]
    decay = np.broadcast_to(np.exp(c * lg)[:, None, None], (RET_HEADS, 1, RET_DV))
    return tuple(np.ascontiguousarray(t, dtype=np.float32) for t in (dmask, xi, zeta, decay))


def kernel(x, w_in, w_ret_out, w_att_out, w_o, w_ffn_gate, w_ffn_up, w_ffn_down,
           g_pre_mix, g_post_mix, g_pre_ffn, g_post_ffn):
    batch, seq, d = x.shape
    depth = w_in.shape[0]
    assert d == D_MODEL and seq % TILE == 0
    cos_r, sin_r, cos_a, sin_a = _rotary_tables(seq)
    consts = _retention_consts()
    xf = x.reshape(batch * seq, d)
    w_ffn_gate, w_ffn_up, w_ffn_down = (
        w.astype(_BF16) for w in (w_ffn_gate, w_ffn_up, w_ffn_down))
    g_pre_mix, g_post_mix, g_pre_ffn, g_post_ffn = (
        g[:, None, :] for g in (g_pre_mix, g_post_mix, g_pre_ffn, g_post_ffn))
    for l in range(depth):
        proj, h = _proj_main(xf, g_pre_mix, w_in, l, cos_r, sin_r)
        att = _proj_att(h, w_in, l, cos_a, sin_a)
        ya = _dilated(att, batch, seq)
        xf = _ret_merge(xf, proj, ya, consts, w_ret_out, w_att_out, w_o, g_post_mix, l, seq)
        xf = _ffn(xf, g_pre_ffn, w_ffn_gate, w_ffn_up, w_ffn_down, g_post_ffn, l)
    return xf.reshape(batch, seq, d)
```
